```python
import math
import jax, jax.numpy as jnp
from jax import lax
import numpy as np

D_MODEL = 1024
BATCH = 4
SEQ = 8192
DEPTH = 2
DEC_BATCH = 8
DEC_SEQ = 32
PAST_LEN = 2048

CHUNK = 64
EPS = 1e-6
SB_HEADS = 8
SB_HEAD_DIM = 64
SB_WIDTH = SB_HEADS * SB_HEAD_DIM
SB_BLOCK = 128
SSD_HEADS = 8
SSD_HEAD_DIM = 64
SSD_INNER = SSD_HEADS * SSD_HEAD_DIM
SSD_GROUPS = 2
SSD_STATE = 64
SSD_CONV = 4
SSD_CONV_DIM = SSD_INNER + 2 * SSD_GROUPS * SSD_STATE
SSD_CHUNK = CHUNK
HYB_SPLITS = (SB_WIDTH, 2 * SB_WIDTH, 3 * SB_WIDTH, 3 * SB_WIDTH + SSD_INNER,
              3 * SB_WIDTH + SSD_INNER + SSD_CONV_DIM)
HYB_IN = 3 * SB_WIDTH + SSD_INNER + SSD_CONV_DIM + SSD_HEADS
RW_HEAD = 64
RW_HEADS = D_MODEL // RW_HEAD
RW_DECAY_LORA = 64
RW_A_LORA = 64
RW_GATE_LORA = 128
RW_LNX_EPS = 64e-5
PEER_HEADS = 8
PEER_NKEYS = 128
PEER_EXPERTS = PEER_NKEYS * PEER_NKEYS
PEER_QDIM = 256
PEER_TOPK = 16
PEER_BLOCK = 128

kernel_name = "stickbreak_ssd_rwkv7_peer_stream_step"

F32 = jnp.float32


def rmsnorm(x, g):
    xf = x.astype(F32)
    y = xf * lax.rsqrt(jnp.mean(xf * xf, axis=-1, keepdims=True) + EPS)
    return (y * g.astype(F32)).astype(x.dtype)


def _sb_block(q, k, v, q_pos, k_pos):
    z = jnp.einsum('bqhd,bshd->bhqs', q, k).astype(F32) * (SB_HEAD_DIM ** -0.5)
    mask = k_pos[None, :] < q_pos[:, None]
    log_fail = jnp.where(mask, jax.nn.log_sigmoid(-z), 0.0)
    later = lax.cumsum(log_fail, axis=3, reverse=True) - log_fail
    log_w = jnp.where(mask, jax.nn.log_sigmoid(z) + later, -jnp.inf)
    w = jnp.exp(log_w).astype(v.dtype)
    return jnp.einsum('bhqs,bshd->bqhd', w, v)


def sb_attention(q, k_all, v_all, past):
    b, lq, h, d = q.shape
    k_pos = jnp.arange(k_all.shape[1], dtype=jnp.int32)
    q_pos = past + jnp.arange(lq, dtype=jnp.int32)
    if lq % SB_BLOCK == 0:
        nb = lq // SB_BLOCK
        qb = q.reshape(b, nb, SB_BLOCK, h, d).transpose(1, 0, 2, 3, 4)
        pb = q_pos.reshape(nb, SB_BLOCK)
        ob = lax.map(lambda a: _sb_block(a[0], k_all, v_all, a[1], k_pos), (qb, pb))
        return ob.transpose(1, 0, 2, 3, 4).reshape(b, lq, h, d)
    return _sb_block(q, k_all, v_all, q_pos, k_pos)


def causal_conv(u, buf, w, bias):
    L = u.shape[1]
    full = jnp.concatenate([buf, u], axis=1)
    out = bias
    for i in range(SSD_CONV):
        out = out + full[:, i:i + L] * w[i]
    return out, full[:, -(SSD_CONV - 1):]


def ssd_scan(x, dt, a, bm, cm, h0):
    b, L, H, P = x.shape
    Q = SSD_CHUNK if L % SSD_CHUNK == 0 else L
    nc = L // Q
    rep = H // SSD_GROUPS
    xf = x.astype(F32).reshape(b, nc, Q, H, P)
    bh = jnp.repeat(bm.astype(F32), rep, axis=2).reshape(b, nc, Q, H, SSD_STATE)
    ch = jnp.repeat(cm.astype(F32), rep, axis=2).reshape(b, nc, Q, H, SSD_STATE)
    dtc = dt.reshape(b, nc, Q, H)
    a_cum = jnp.cumsum(dtc * a, axis=2)
    seg = a_cum[:, :, :, None, :] - a_cum[:, :, None, :, :]
    causal = jnp.tril(jnp.ones((Q, Q), dtype=bool))[None, None, :, :, None]
    decay = jnp.exp(jnp.where(causal, seg, -jnp.inf))
    cb = jnp.einsum('bclhn,bcshn->bclsh', ch, bh)
    y_diag = jnp.einsum('bclsh,bcshp->bclhp', cb * decay * dtc[:, :, None], xf)
    w_state = jnp.exp(a_cum[:, :, -1:] - a_cum) * dtc
    chunk_states = jnp.einsum('bclhn,bclh,bclhp->bchpn', bh, w_state, xf)
    chunk_decay = jnp.exp(a_cum[:, :, -1])

    def step(h, inp):
        dec, st = inp
        return dec[:, :, None, None] * h + st, h

    h_last, h_prev = lax.scan(step, h0, (chunk_decay.transpose(1, 0, 2),
                                          chunk_states.transpose(1, 0, 2, 3, 4)))
    h_prev = h_prev.transpose(1, 0, 2, 3, 4)
    y_off = jnp.einsum('bclhn,bchpn->bclhp', ch, h_prev) * jnp.exp(a_cum)[..., None]
    return (y_diag + y_off).reshape(b, L, H, P), h_last


def hybrid_mixer(xn, k_cache, v_cache, conv_buf, ssd_h,
                 w_in, conv_w, conv_b, dt_bias, a_log, d_skip, ssd_norm, w_out):
    b, L, _ = xn.shape
    proj = xn @ w_in
    q, k, v, z, xbc, dt = jnp.split(proj, HYB_SPLITS, axis=-1)
    q = q.reshape(b, L, SB_HEADS, SB_HEAD_DIM)
    k = k.reshape(b, L, SB_HEADS, SB_HEAD_DIM)
    v = v.reshape(b, L, SB_HEADS, SB_HEAD_DIM)
    past = k_cache.shape[1]
    k_all = jnp.concatenate([k_cache, k], axis=1)
    v_all = jnp.concatenate([v_cache, v], axis=1)
    o_sb = sb_attention(q, k_all, v_all, past).reshape(b, L, SB_WIDTH)
    xbc_c, new_buf = causal_conv(xbc, conv_buf, conv_w, conv_b)
    xbc_c = jax.nn.silu(xbc_c)
    xs, bm, cm = jnp.split(xbc_c, (SSD_INNER, SSD_INNER + SSD_GROUPS * SSD_STATE), axis=-1)
    xs = xs.reshape(b, L, SSD_HEADS, SSD_HEAD_DIM)
    bm = bm.reshape(b, L, SSD_GROUPS, SSD_STATE)
    cm = cm.reshape(b, L, SSD_GROUPS, SSD_STATE)
    dtp = jax.nn.softplus(dt.astype(F32) + dt_bias.astype(F32))
    a = -jnp.exp(a_log.astype(F32))
    y, h_new = ssd_scan(xs, dtp, a, bm, cm, ssd_h.astype(F32))
    y = y + d_skip.astype(F32)[:, None] * xs.astype(F32)
    y = y.reshape(b, L, SSD_INNER) * jax.nn.silu(z.astype(F32))
    yg = y.reshape(b, L, SSD_GROUPS, SSD_INNER // SSD_GROUPS)
    yg = yg * lax.rsqrt(jnp.mean(yg * yg, axis=-1, keepdims=True) + EPS)
    y = yg.reshape(b, L, SSD_INNER) * ssd_norm.astype(F32)
    out = jnp.concatenate([o_sb, y.astype(xn.dtype)], axis=-1) @ w_out
    return out, k, v, new_buf, h_new.astype(ssd_h.dtype)


def rwkv7_mixer(xn, shift_buf, wkv, mu, w_rkv, w0, w1, w2, a0, a1, a2, g1, g2,
                k_k, k_a, r_k, lnx_w, lnx_b, w_out):
    b, L, D = xn.shape
    x_prev = jnp.concatenate([shift_buf, xn[:, :-1]], axis=1)
    xx = x_prev - xn
    xm = xn[None] + xx[None] * mu[:, None, None, :]
    r, k, v = jnp.einsum('ibld,ide->ible', xm[:3], w_rkv).astype(F32)
    xw, xa, xg = xm[3], xm[4], xm[5]
    w = -jax.nn.softplus(-(w0 + jnp.tanh(xw @ w1) @ w2).astype(F32)) - 0.5
    decay = jnp.exp(-jnp.exp(w))
    a = jax.nn.sigmoid((a0 + (xa @ a1) @ a2).astype(F32))
    g = (jax.nn.sigmoid(xg @ g1) @ g2).astype(F32)
    hs = (b, L, RW_HEADS, RW_HEAD)
    kk = (k * k_k.astype(F32)).reshape(hs)
    kk = kk / jnp.maximum(jnp.sqrt(jnp.sum(kk * kk, axis=-1, keepdims=True)), 1e-12)
    k = k * (1.0 + (a - 1.0) * k_a.astype(F32))
    rh, dh, kh, vh, ah = (t.reshape(hs) for t in (r, decay, k, v, a))
    a_vec = -kk
    b_vec = kk * ah

    def step(S, inp):
        r_t, d_t, k_t, v_t, a_t, b_t = inp
        sa = jnp.einsum('bhvk,bhk->bhv', S, a_t)
        S = (S * d_t[:, :, None, :] + sa[..., None] * b_t[:, :, None, :]
             + v_t[..., None] * k_t[:, :, None, :])
        return S, jnp.einsum('bhvk,bhk->bhv', S, r_t)

    seqs = tuple(t.transpose(1, 0, 2, 3) for t in (rh, dh, kh, vh, a_vec, b_vec))
    S_last, o = lax.scan(step, wkv.astype(F32), seqs)
    o = o.transpose(1, 0, 2, 3)
    mean = jnp.mean(o, axis=-1, keepdims=True)
    var = jnp.mean(jnp.square(o - mean), axis=-1, keepdims=True)
    o = ((o - mean) * lax.rsqrt(var + RW_LNX_EPS)).reshape(b, L, D)
    o = o * lnx_w.astype(F32) + lnx_b.astype(F32)
    bonus = jnp.sum(rh * kh * r_k.astype(F32), axis=-1, keepdims=True) * vh
    o = o + bonus.reshape(b, L, D)
    out = (o * g).astype(xn.dtype) @ w_out
    return out, xn[:, -1:], S_last.astype(wkv.dtype)


def peer_ffn(xn, w_q, keys, u_tab, v_tab):
    b, L, D = xn.shape
    t = b * L
    pad = (-t) % PEER_BLOCK
    xf = jnp.pad(xn.reshape(t, D), ((0, pad), (0, 0)))
    nb = (t + pad) // PEER_BLOCK

    def block(xb):
        q = (xb @ w_q).reshape(PEER_BLOCK, PEER_HEADS, 2, PEER_QDIM // 2)
        s = jnp.einsum('thic,hikc->thik', q, keys).astype(F32)
        s1, i1 = lax.top_k(s[:, :, 0], PEER_TOPK)
        s2, i2 = lax.top_k(s[:, :, 1], PEER_TOPK)
        cand = (s1[..., :, None] + s2[..., None, :]).reshape(PEER_BLOCK, PEER_HEADS, PEER_TOPK * PEER_TOPK)
        cand_id = (i1[..., :, None] * PEER_NKEYS + i2[..., None, :]).reshape(PEER_BLOCK, PEER_HEADS, PEER_TOPK * PEER_TOPK)
        top, pos = lax.top_k(cand, PEER_TOPK)
        eid = jnp.take_along_axis(cand_id, pos, axis=-1)
        gate = jax.nn.softmax(top, axis=-1)
        hidden = jnp.einsum('thkd,td->thk', u_tab[eid], xb).astype(F32)
        act = (jax.nn.gelu(hidden, approximate=False) * gate).astype(xb.dtype)
        return jnp.einsum('thk,thkd->td', act, v_tab[eid])

    out = lax.map(block, xf.reshape(nb, PEER_BLOCK, D))
    return out.reshape(nb * PEER_BLOCK, D)[:t].reshape(b, L, D)


def run_trunk(x, states, layer_params, peer_params, final_norm):
    new_states = []
    for l in range(DEPTH):
        mp = layer_params[l]
        xn = rmsnorm(x, mp[0])
        if l % 2 == 0:
            out, *ns = hybrid_mixer(xn, *states[l], *mp[1:])
        else:
            out, *ns = rwkv7_mixer(xn, *states[l], *mp[1:])
        x = x + out
        ng, wq, keys, u, v = peer_params[l]
        x = x + peer_ffn(rmsnorm(x, ng), wq, keys, u, v)
        new_states.extend(ns)
    return rmsnorm(x, final_norm), new_states


def setup_inputs(seed: int = 0) -> dict:
    key = jax.random.key(seed)
    ks = iter(jax.random.split(key, 80))
    D = D_MODEL

    def nrm(shape, scale):
        return jax.random.normal(next(ks), shape, F32) * scale

    def uni(shape, lo, hi):
        return jax.random.uniform(next(ks), shape, F32, minval=lo, maxval=hi)

    def gain(n):
        return 1.0 + nrm((n,), 0.02)

    inp = {}
    inp['x_prompt'] = nrm((BATCH, SEQ, D), 1.0)
    inp['x_sample'] = nrm((DEC_BATCH, DEC_SEQ, D), 1.0)
    inp['cache_sb_k'] = nrm((DEC_BATCH, PAST_LEN, SB_HEADS, SB_HEAD_DIM), 1.0)
    inp['cache_sb_v'] = nrm((DEC_BATCH, PAST_LEN, SB_HEADS, SB_HEAD_DIM), 1.0)
    inp['state_ssd_conv'] = nrm((DEC_BATCH, SSD_CONV - 1, SSD_CONV_DIM), 1.0)
    inp['state_ssd'] = nrm((DEC_BATCH, SSD_HEADS, SSD_HEAD_DIM, SSD_STATE), 0.1)
    inp['state_rwkv_shift'] = nrm((DEC_BATCH, 1, D), 1.0)
    inp['state_rwkv_wkv'] = nrm((DEC_BATCH, RW_HEADS, RW_HEAD, RW_HEAD), 0.3)
    inp['l0_norm_mix'] = gain(D)
    inp['l0_w_in'] = nrm((D, HYB_IN), D ** -0.5)
    inp['l0_conv_w'] = nrm((SSD_CONV, SSD_CONV_DIM), SSD_CONV ** -0.5)
    inp['l0_conv_b'] = nrm((SSD_CONV_DIM,), 0.02)
    dt0 = jnp.exp(uni((SSD_HEADS,), math.log(1e-3), math.log(1e-1)))
    inp['l0_dt_bias'] = dt0 + jnp.log(-jnp.expm1(-dt0))
    inp['l0_a_log'] = jnp.log(uni((SSD_HEADS,), 1.0, 16.0))
    inp['l0_d_skip'] = gain(SSD_HEADS)
    inp['l0_ssd_norm'] = gain(SSD_INNER)
    inp['l0_w_out'] = nrm((SB_WIDTH + SSD_INNER, D), (SB_WIDTH + SSD_INNER) ** -0.5)
    inp['l1_norm_mix'] = gain(D)
    inp['l1_mu'] = uni((6, D), 0.0, 1.0)
    inp['l1_w_rkv'] = nrm((3, D, D), D ** -0.5)
    inp['l1_w0'] = uni((D,), -6.0, -1.0)
    inp['l1_w1'] = nrm((D, RW_DECAY_LORA), D ** -0.5)
    inp['l1_w2'] = nrm((RW_DECAY_LORA, D), 0.1)
    inp['l1_a0'] = nrm((D,), 0.1)
    inp['l1_a1'] = nrm((D, RW_A_LORA), D ** -0.5)
    inp['l1_a2'] = nrm((RW_A_LORA, D), 0.5 * RW_A_LORA ** -0.5)
    inp['l1_g1'] = nrm((D, RW_GATE_LORA), D ** -0.5)
    inp['l1_g2'] = nrm((RW_GATE_LORA, D), RW_GATE_LORA ** -0.5)
    inp['l1_k_k'] = 0.85 + nrm((D,), 0.02)
    inp['l1_k_a'] = gain(D)
    inp['l1_r_k'] = nrm((RW_HEADS, RW_HEAD), 0.1)
    inp['l1_lnx_w'] = gain(D)
    inp['l1_lnx_b'] = nrm((D,), 0.02)
    inp['l1_w_out'] = nrm((D, D), D ** -0.5)
    for l in range(DEPTH):
        inp[f'l{l}_norm_ffn'] = gain(D)
        inp[f'l{l}_peer_wq'] = nrm((D, PEER_HEADS * PEER_QDIM), D ** -0.5)
        inp[f'l{l}_peer_keys'] = nrm((PEER_HEADS, 2, PEER_NKEYS, PEER_QDIM // 2), (PEER_QDIM // 2) ** -0.5)
        inp[f'l{l}_peer_u'] = nrm((PEER_EXPERTS, D), D ** -0.5)
        inp[f'l{l}_peer_v'] = nrm((PEER_EXPERTS, D), 0.25)
    inp['final_norm'] = gain(D)
    return inp


def reference(x_prompt, x_sample, cache_sb_k, cache_sb_v, state_ssd_conv, state_ssd,
              state_rwkv_shift, state_rwkv_wkv,
              l0_norm_mix, l0_w_in, l0_conv_w, l0_conv_b, l0_dt_bias, l0_a_log, l0_d_skip,
              l0_ssd_norm, l0_w_out,
              l1_norm_mix, l1_mu, l1_w_rkv, l1_w0, l1_w1, l1_w2, l1_a0, l1_a1, l1_a2,
              l1_g1, l1_g2, l1_k_k, l1_k_a, l1_r_k, l1_lnx_w, l1_lnx_b, l1_w_out,
              l0_norm_ffn, l0_peer_wq, l0_peer_keys, l0_peer_u, l0_peer_v,
              l1_norm_ffn, l1_peer_wq, l1_peer_keys, l1_peer_u, l1_peer_v,
              final_norm):
    layer_params = [
        (l0_norm_mix, l0_w_in, l0_conv_w, l0_conv_b, l0_dt_bias, l0_a_log, l0_d_skip,
         l0_ssd_norm, l0_w_out),
        (l1_norm_mix, l1_mu, l1_w_rkv, l1_w0, l1_w1, l1_w2, l1_a0, l1_a1, l1_a2,
         l1_g1, l1_g2, l1_k_k, l1_k_a, l1_r_k, l1_lnx_w, l1_lnx_b, l1_w_out),
    ]
    peer_params = [
        (l0_norm_ffn, l0_peer_wq, l0_peer_keys, l0_peer_u, l0_peer_v),
        (l1_norm_ffn, l1_peer_wq, l1_peer_keys, l1_peer_u, l1_peer_v),
    ]
    bp = x_prompt.shape[0]
    dt = x_prompt.dtype
    zero_states = [
        (jnp.zeros((bp, 0, SB_HEADS, SB_HEAD_DIM), dt), jnp.zeros((bp, 0, SB_HEADS, SB_HEAD_DIM), dt),
         jnp.zeros((bp, SSD_CONV - 1, SSD_CONV_DIM), dt),
         jnp.zeros((bp, SSD_HEADS, SSD_HEAD_DIM, SSD_STATE), dt)),
        (jnp.zeros((bp, 1, D_MODEL), dt), jnp.zeros((bp, RW_HEADS, RW_HEAD, RW_HEAD), dt)),
    ]
    y_prompt, (p_sb_k, p_sb_v, p_conv, p_ssd, p_shift, p_wkv) = run_trunk(
        x_prompt, zero_states, layer_params, peer_params, final_norm)
    sample_states = [
        (cache_sb_k, cache_sb_v, state_ssd_conv, state_ssd),
        (state_rwkv_shift, state_rwkv_wkv),
    ]
    y_sample, (s_sb_k, s_sb_v, s_conv, s_ssd, s_shift, s_wkv) = run_trunk(
        x_sample, sample_states, layer_params, peer_params, final_norm)
    return (y_prompt, y_sample, p_sb_k, p_sb_v, p_conv, p_ssd, p_shift, p_wkv,
            s_sb_k, s_sb_v, s_conv, s_ssd, s_shift, s_wkv)
```

```python
import math
import jax, jax.numpy as jnp
from jax import lax
import numpy as np
from jax.experimental import pallas as pl
from jax.experimental.pallas import tpu as pltpu

D_MODEL = 1024
DEPTH = 2

CHUNK = 64
EPS = 1e-6
SB_HEADS = 8
SB_HEAD_DIM = 64
SB_WIDTH = SB_HEADS * SB_HEAD_DIM
SB_BLOCK = 128
SSD_HEADS = 8
SSD_HEAD_DIM = 64
SSD_INNER = SSD_HEADS * SSD_HEAD_DIM
SSD_GROUPS = 2
SSD_STATE = 64
SSD_CONV = 4
SSD_CONV_DIM = SSD_INNER + 2 * SSD_GROUPS * SSD_STATE
SSD_CHUNK = CHUNK
HYB_SPLITS = (SB_WIDTH, 2 * SB_WIDTH, 3 * SB_WIDTH, 3 * SB_WIDTH + SSD_INNER,
              3 * SB_WIDTH + SSD_INNER + SSD_CONV_DIM)
HYB_IN = 3 * SB_WIDTH + SSD_INNER + SSD_CONV_DIM + SSD_HEADS
RW_HEAD = 64
RW_HEADS = D_MODEL // RW_HEAD
RW_LNX_EPS = 64e-5
PEER_HEADS = 8
PEER_NKEYS = 128
PEER_QDIM = 256
PEER_TOPK = 16
PEER_BLOCK = 128

F32 = jnp.float32


def rmsnorm(x, g):
    xf = x.astype(F32)
    y = xf * lax.rsqrt(jnp.mean(xf * xf, axis=-1, keepdims=True) + EPS)
    return (y * g.astype(F32)).astype(x.dtype)


def _rmsnorm_kernel(x_ref, g_ref, o_ref):
    x = x_ref[...]
    y = x * lax.rsqrt(jnp.mean(x * x, axis=-1, keepdims=True) + EPS)
    o_ref[...] = y * g_ref[...]


def pallas_rmsnorm(x, g, rows=256):
    shp = x.shape
    x2 = x.reshape(-1, shp[-1])
    t, d = x2.shape
    rows = min(rows, t)
    out = pl.pallas_call(
        _rmsnorm_kernel,
        grid=(t // rows,),
        in_specs=[pl.BlockSpec((rows, d), lambda i: (i, 0)),
                  pl.BlockSpec((1, d), lambda i: (0, 0))],
        out_specs=pl.BlockSpec((rows, d), lambda i: (i, 0)),
        out_shape=jax.ShapeDtypeStruct((t, d), x.dtype),
        name="final_rmsnorm",
    )(x2, g.reshape(1, d))
    return out.reshape(shp)


def _sb_block(q, k, v, q_pos, k_pos):
    z = jnp.einsum('bqhd,bshd->bhqs', q, k).astype(F32) * (SB_HEAD_DIM ** -0.5)
    mask = k_pos[None, :] < q_pos[:, None]
    log_fail = jnp.where(mask, jax.nn.log_sigmoid(-z), 0.0)
    later = lax.cumsum(log_fail, axis=3, reverse=True) - log_fail
    log_w = jnp.where(mask, jax.nn.log_sigmoid(z) + later, -jnp.inf)
    w = jnp.exp(log_w).astype(v.dtype)
    return jnp.einsum('bhqs,bshd->bqhd', w, v)


def sb_attention(q, k_all, v_all, past):
    b, lq, h, d = q.shape
    k_pos = jnp.arange(k_all.shape[1], dtype=jnp.int32)
    q_pos = past + jnp.arange(lq, dtype=jnp.int32)
    if lq % SB_BLOCK == 0:
        nb = lq // SB_BLOCK
        qb = q.reshape(b, nb, SB_BLOCK, h, d).transpose(1, 0, 2, 3, 4)
        pb = q_pos.reshape(nb, SB_BLOCK)
        ob = lax.map(lambda a: _sb_block(a[0], k_all, v_all, a[1], k_pos), (qb, pb))
        return ob.transpose(1, 0, 2, 3, 4).reshape(b, lq, h, d)
    return _sb_block(q, k_all, v_all, q_pos, k_pos)


def causal_conv(u, buf, w, bias):
    L = u.shape[1]
    full = jnp.concatenate([buf, u], axis=1)
    out = bias
    for i in range(SSD_CONV):
        out = out + full[:, i:i + L] * w[i]
    return out, full[:, -(SSD_CONV - 1):]


def ssd_scan(x, dt, a, bm, cm, h0):
    b, L, H, P = x.shape
    Q = SSD_CHUNK if L % SSD_CHUNK == 0 else L
    nc = L // Q
    rep = H // SSD_GROUPS
    xf = x.astype(F32).reshape(b, nc, Q, H, P)
    bh = jnp.repeat(bm.astype(F32), rep, axis=2).reshape(b, nc, Q, H, SSD_STATE)
    ch = jnp.repeat(cm.astype(F32), rep, axis=2).reshape(b, nc, Q, H, SSD_STATE)
    dtc = dt.reshape(b, nc, Q, H)
    a_cum = jnp.cumsum(dtc * a, axis=2)
    seg = a_cum[:, :, :, None, :] - a_cum[:, :, None, :, :]
    causal = jnp.tril(jnp.ones((Q, Q), dtype=bool))[None, None, :, :, None]
    decay = jnp.exp(jnp.where(causal, seg, -jnp.inf))
    cb = jnp.einsum('bclhn,bcshn->bclsh', ch, bh)
    y_diag = jnp.einsum('bclsh,bcshp->bclhp', cb * decay * dtc[:, :, None], xf)
    w_state = jnp.exp(a_cum[:, :, -1:] - a_cum) * dtc
    chunk_states = jnp.einsum('bclhn,bclh,bclhp->bchpn', bh, w_state, xf)
    chunk_decay = jnp.exp(a_cum[:, :, -1])

    def step(h, inp):
        dec, st = inp
        return dec[:, :, None, None] * h + st, h

    h_last, h_prev = lax.scan(step, h0, (chunk_decay.transpose(1, 0, 2),
                                          chunk_states.transpose(1, 0, 2, 3, 4)))
    h_prev = h_prev.transpose(1, 0, 2, 3, 4)
    y_off = jnp.einsum('bclhn,bchpn->bclhp', ch, h_prev) * jnp.exp(a_cum)[..., None]
    return (y_diag + y_off).reshape(b, L, H, P), h_last


def hybrid_mixer(xn, k_cache, v_cache, conv_buf, ssd_h,
                 w_in, conv_w, conv_b, dt_bias, a_log, d_skip, ssd_norm, w_out):
    b, L, _ = xn.shape
    proj = xn @ w_in
    q, k, v, z, xbc, dt = jnp.split(proj, HYB_SPLITS, axis=-1)
    q = q.reshape(b, L, SB_HEADS, SB_HEAD_DIM)
    k = k.reshape(b, L, SB_HEADS, SB_HEAD_DIM)
    v = v.reshape(b, L, SB_HEADS, SB_HEAD_DIM)
    past = k_cache.shape[1]
    k_all = jnp.concatenate([k_cache, k], axis=1)
    v_all = jnp.concatenate([v_cache, v], axis=1)
    o_sb = sb_attention(q, k_all, v_all, past).reshape(b, L, SB_WIDTH)
    xbc_c, new_buf = causal_conv(xbc, conv_buf, conv_w, conv_b)
    xbc_c = jax.nn.silu(xbc_c)
    xs, bm, cm = jnp.split(xbc_c, (SSD_INNER, SSD_INNER + SSD_GROUPS * SSD_STATE), axis=-1)
    xs = xs.reshape(b, L, SSD_HEADS, SSD_HEAD_DIM)
    bm = bm.reshape(b, L, SSD_GROUPS, SSD_STATE)
    cm = cm.reshape(b, L, SSD_GROUPS, SSD_STATE)
    dtp = jax.nn.softplus(dt.astype(F32) + dt_bias.astype(F32))
    a = -jnp.exp(a_log.astype(F32))
    y, h_new = ssd_scan(xs, dtp, a, bm, cm, ssd_h.astype(F32))
    y = y + d_skip.astype(F32)[:, None] * xs.astype(F32)
    y = y.reshape(b, L, SSD_INNER) * jax.nn.silu(z.astype(F32))
    yg = y.reshape(b, L, SSD_GROUPS, SSD_INNER // SSD_GROUPS)
    yg = yg * lax.rsqrt(jnp.mean(yg * yg, axis=-1, keepdims=True) + EPS)
    y = yg.reshape(b, L, SSD_INNER) * ssd_norm.astype(F32)
    out = jnp.concatenate([o_sb, y.astype(xn.dtype)], axis=-1) @ w_out
    return out, k, v, new_buf, h_new.astype(ssd_h.dtype)


def rwkv7_mixer(xn, shift_buf, wkv, mu, w_rkv, w0, w1, w2, a0, a1, a2, g1, g2,
                k_k, k_a, r_k, lnx_w, lnx_b, w_out):
    b, L, D = xn.shape
    x_prev = jnp.concatenate([shift_buf, xn[:, :-1]], axis=1)
    xx = x_prev - xn
    xm = xn[None] + xx[None] * mu[:, None, None, :]
    r, k, v = jnp.einsum('ibld,ide->ible', xm[:3], w_rkv).astype(F32)
    xw, xa, xg = xm[3], xm[4], xm[5]
    w = -jax.nn.softplus(-(w0 + jnp.tanh(xw @ w1) @ w2).astype(F32)) - 0.5
    decay = jnp.exp(-jnp.exp(w))
    a = jax.nn.sigmoid((a0 + (xa @ a1) @ a2).astype(F32))
    g = (jax.nn.sigmoid(xg @ g1) @ g2).astype(F32)
    hs = (b, L, RW_HEADS, RW_HEAD)
    kk = (k * k_k.astype(F32)).reshape(hs)
    kk = kk / jnp.maximum(jnp.sqrt(jnp.sum(kk * kk, axis=-1, keepdims=True)), 1e-12)
    k = k * (1.0 + (a - 1.0) * k_a.astype(F32))
    rh, dh, kh, vh, ah = (t.reshape(hs) for t in (r, decay, k, v, a))
    a_vec = -kk
    b_vec = kk * ah

    def step(S, inp):
        r_t, d_t, k_t, v_t, a_t, b_t = inp
        sa = jnp.einsum('bhvk,bhk->bhv', S, a_t)
        S = (S * d_t[:, :, None, :] + sa[..., None] * b_t[:, :, None, :]
             + v_t[..., None] * k_t[:, :, None, :])
        return S, jnp.einsum('bhvk,bhk->bhv', S, r_t)

    seqs = tuple(t.transpose(1, 0, 2, 3) for t in (rh, dh, kh, vh, a_vec, b_vec))
    S_last, o = lax.scan(step, wkv.astype(F32), seqs)
    o = o.transpose(1, 0, 2, 3)
    mean = jnp.mean(o, axis=-1, keepdims=True)
    var = jnp.mean(jnp.square(o - mean), axis=-1, keepdims=True)
    o = ((o - mean) * lax.rsqrt(var + RW_LNX_EPS)).reshape(b, L, D)
    o = o * lnx_w.astype(F32) + lnx_b.astype(F32)
    bonus = jnp.sum(rh * kh * r_k.astype(F32), axis=-1, keepdims=True) * vh
    o = o + bonus.reshape(b, L, D)
    out = (o * g).astype(xn.dtype) @ w_out
    return out, xn[:, -1:], S_last.astype(wkv.dtype)


def peer_ffn(xn, w_q, keys, u_tab, v_tab):
    b, L, D = xn.shape
    t = b * L
    pad = (-t) % PEER_BLOCK
    xf = jnp.pad(xn.reshape(t, D), ((0, pad), (0, 0)))
    nb = (t + pad) // PEER_BLOCK

    def block(xb):
        q = (xb @ w_q).reshape(PEER_BLOCK, PEER_HEADS, 2, PEER_QDIM // 2)
        s = jnp.einsum('thic,hikc->thik', q, keys).astype(F32)
        s1, i1 = lax.top_k(s[:, :, 0], PEER_TOPK)
        s2, i2 = lax.top_k(s[:, :, 1], PEER_TOPK)
        cand = (s1[..., :, None] + s2[..., None, :]).reshape(PEER_BLOCK, PEER_HEADS, PEER_TOPK * PEER_TOPK)
        cand_id = (i1[..., :, None] * PEER_NKEYS + i2[..., None, :]).reshape(PEER_BLOCK, PEER_HEADS, PEER_TOPK * PEER_TOPK)
        top, pos = lax.top_k(cand, PEER_TOPK)
        eid = jnp.take_along_axis(cand_id, pos, axis=-1)
        gate = jax.nn.softmax(top, axis=-1)
        hidden = jnp.einsum('thkd,td->thk', u_tab[eid], xb).astype(F32)
        act = (jax.nn.gelu(hidden, approximate=False) * gate).astype(xb.dtype)
        return jnp.einsum('thk,thkd->td', act, v_tab[eid])

    out = lax.map(block, xf.reshape(nb, PEER_BLOCK, D))
    return out.reshape(nb * PEER_BLOCK, D)[:t].reshape(b, L, D)


def run_trunk(x, states, layer_params, peer_params, final_norm):
    new_states = []
    for l in range(DEPTH):
        mp = layer_params[l]
        xn = rmsnorm(x, mp[0])
        if l % 2 == 0:
            out, *ns = hybrid_mixer(xn, *states[l], *mp[1:])
        else:
            out, *ns = rwkv7_mixer(xn, *states[l], *mp[1:])
        x = x + out
        ng, wq, keys, u, v = peer_params[l]
        x = x + peer_ffn(rmsnorm(x, ng), wq, keys, u, v)
        new_states.extend(ns)
    return pallas_rmsnorm(x, final_norm), new_states


def kernel(x_prompt, x_sample, cache_sb_k, cache_sb_v, state_ssd_conv, state_ssd,
           state_rwkv_shift, state_rwkv_wkv,
           l0_norm_mix, l0_w_in, l0_conv_w, l0_conv_b, l0_dt_bias, l0_a_log, l0_d_skip,
           l0_ssd_norm, l0_w_out,
           l1_norm_mix, l1_mu, l1_w_rkv, l1_w0, l1_w1, l1_w2, l1_a0, l1_a1, l1_a2,
           l1_g1, l1_g2, l1_k_k, l1_k_a, l1_r_k, l1_lnx_w, l1_lnx_b, l1_w_out,
           l0_norm_ffn, l0_peer_wq, l0_peer_keys, l0_peer_u, l0_peer_v,
           l1_norm_ffn, l1_peer_wq, l1_peer_keys, l1_peer_u, l1_peer_v,
           final_norm):
    layer_params = [
        (l0_norm_mix, l0_w_in, l0_conv_w, l0_conv_b, l0_dt_bias, l0_a_log, l0_d_skip,
         l0_ssd_norm, l0_w_out),
        (l1_norm_mix, l1_mu, l1_w_rkv, l1_w0, l1_w1, l1_w2, l1_a0, l1_a1, l1_a2,
         l1_g1, l1_g2, l1_k_k, l1_k_a, l1_r_k, l1_lnx_w, l1_lnx_b, l1_w_out),
    ]
    peer_params = [
        (l0_norm_ffn, l0_peer_wq, l0_peer_keys, l0_peer_u, l0_peer_v),
        (l1_norm_ffn, l1_peer_wq, l1_peer_keys, l1_peer_u, l1_peer_v),
    ]
    bp = x_prompt.shape[0]
    dt = x_prompt.dtype
    zero_states = [
        (jnp.zeros((bp, 0, SB_HEADS, SB_HEAD_DIM), dt), jnp.zeros((bp, 0, SB_HEADS, SB_HEAD_DIM), dt),
         jnp.zeros((bp, SSD_CONV - 1, SSD_CONV_DIM), dt),
         jnp.zeros((bp, SSD_HEADS, SSD_HEAD_DIM, SSD_STATE), dt)),
        (jnp.zeros((bp, 1, D_MODEL), dt), jnp.zeros((bp, RW_HEADS, RW_HEAD, RW_HEAD), dt)),
    ]
    y_prompt, (p_sb_k, p_sb_v, p_conv, p_ssd, p_shift, p_wkv) = run_trunk(
        x_prompt, zero_states, layer_params, peer_params, final_norm)
    sample_states = [
        (cache_sb_k, cache_sb_v, state_ssd_conv, state_ssd),
        (state_rwkv_shift, state_rwkv_wkv),
    ]
    y_sample, (s_sb_k, s_sb_v, s_conv, s_ssd, s_shift, s_wkv) = run_trunk(
        x_sample, sample_states, layer_params, peer_params, final_norm)
    return (y_prompt, y_sample, p_sb_k, p_sb_v, p_conv, p_ssd, p_shift, p_wkv,
            s_sb_k, s_sb_v, s_conv, s_ssd, s_shift, s_wkv)
```

```python
import functools
import math
import jax, jax.numpy as jnp
from jax import lax
import numpy as np
from jax.experimental import pallas as pl
from jax.experimental.pallas import tpu as pltpu

D_MODEL = 1024
DEPTH = 2

CHUNK = 64
EPS = 1e-6
SB_HEADS = 8
SB_HEAD_DIM = 64
SB_WIDTH = SB_HEADS * SB_HEAD_DIM
SB_BLOCK = 128
SSD_HEADS = 8
SSD_HEAD_DIM = 64
SSD_INNER = SSD_HEADS * SSD_HEAD_DIM
SSD_GROUPS = 2
SSD_STATE = 64
SSD_CONV = 4
SSD_CONV_DIM = SSD_INNER + 2 * SSD_GROUPS * SSD_STATE
SSD_CHUNK = CHUNK
HYB_SPLITS = (SB_WIDTH, 2 * SB_WIDTH, 3 * SB_WIDTH, 3 * SB_WIDTH + SSD_INNER,
              3 * SB_WIDTH + SSD_INNER + SSD_CONV_DIM)
HYB_IN = 3 * SB_WIDTH + SSD_INNER + SSD_CONV_DIM + SSD_HEADS
RW_HEAD = 64
RW_HEADS = D_MODEL // RW_HEAD
RW_LNX_EPS = 64e-5
PEER_HEADS = 8
PEER_NKEYS = 128
PEER_QDIM = 256
PEER_TOPK = 16
PEER_BLOCK = 128

F32 = jnp.float32


def rmsnorm(x, g):
    xf = x.astype(F32)
    y = xf * lax.rsqrt(jnp.mean(xf * xf, axis=-1, keepdims=True) + EPS)
    return (y * g.astype(F32)).astype(x.dtype)


def _rmsnorm_kernel(x_ref, g_ref, o_ref):
    x = x_ref[...]
    y = x * lax.rsqrt(jnp.mean(x * x, axis=-1, keepdims=True) + EPS)
    o_ref[...] = y * g_ref[...]


def pallas_rmsnorm(x, g, rows=256):
    shp = x.shape
    x2 = x.reshape(-1, shp[-1])
    t, d = x2.shape
    rows = min(rows, t)
    out = pl.pallas_call(
        _rmsnorm_kernel,
        grid=(t // rows,),
        in_specs=[pl.BlockSpec((rows, d), lambda i: (i, 0)),
                  pl.BlockSpec((1, d), lambda i: (0, 0))],
        out_specs=pl.BlockSpec((rows, d), lambda i: (i, 0)),
        out_shape=jax.ShapeDtypeStruct((t, d), x.dtype),
        name="final_rmsnorm",
    )(x2, g.reshape(1, d))
    return out.reshape(shp)


SB_KBLOCK = 128
SB_LOG_FLOOR = -90.0


def _split3(x):
    h = x.astype(jnp.bfloat16)
    r = x - h.astype(F32)
    m = r.astype(jnp.bfloat16)
    l = (r - m.astype(F32)).astype(jnp.bfloat16)
    return h, m, l


def _sb_attn_kernel(q_ref, k_ref, v_ref, o_ref, *, tq, past):
    iq = pl.program_id(2)
    q = q_ref[0, 0]
    q_pos = past + iq * tq + lax.broadcasted_iota(jnp.int32, (tq, SB_KBLOCK), 0)
    k_off = lax.broadcasted_iota(jnp.int32, (tq, SB_KBLOCK), 1)
    r = lax.broadcasted_iota(jnp.int32, (SB_KBLOCK, SB_KBLOCK), 0)
    c = lax.broadcasted_iota(jnp.int32, (SB_KBLOCK, SB_KBLOCK), 1)
    upper = (r > c).astype(jnp.bfloat16)
    j0 = (past + (iq + 1) * tq - 1) // SB_KBLOCK

    def cond(state):
        j, live, _, _ = state
        return jnp.logical_and(j >= 0, live > 0)

    def body(state):
        j, _, carry, acc = state
        start = pl.multiple_of(j * SB_KBLOCK, SB_KBLOCK)
        kb = k_ref[0, 0, pl.ds(start, SB_KBLOCK), :]
        vb = v_ref[0, 0, pl.ds(start, SB_KBLOCK), :]
        z = lax.dot_general(q, kb, (((1,), (1,)), ((), ())), preferred_element_type=F32)
        mask = (j * SB_KBLOCK + k_off) < q_pos
        sp = jnp.maximum(z, 0.0) + jnp.log1p(jnp.exp(-jnp.abs(z)))
        log_fail = jnp.where(mask, -sp, 0.0)
        h, m, l = _split3(log_fail)
        later = (jnp.dot(h, upper, preferred_element_type=F32)
                 + jnp.dot(m, upper, preferred_element_type=F32)
                 + jnp.dot(l, upper, preferred_element_type=F32))
        log_w = (z - sp) + later + carry
        w = jnp.where(mask, jnp.exp(log_w), 0.0)
        acc = acc + jnp.dot(w.astype(jnp.bfloat16), vb, preferred_element_type=F32)
        carry = carry + jnp.sum(log_fail, axis=-1, keepdims=True)
        live = (jnp.max(carry) > SB_LOG_FLOOR).astype(jnp.int32)
        return j - 1, live, carry, acc

    state = (j0, jnp.int32(1), jnp.zeros((tq, 1), F32), jnp.zeros((tq, SB_HEAD_DIM), F32))
    _, _, _, acc = lax.while_loop(cond, body, state)
    o_ref[0, 0] = acc


def sb_attention(q, k_all, v_all, past):
    b, lq, h, d = q.shape
    s = k_all.shape[1]
    tq = min(lq, 128)
    s_pad = -(-s // SB_KBLOCK) * SB_KBLOCK
    qh = (q * (d ** -0.5)).astype(jnp.bfloat16).transpose(0, 2, 1, 3)
    kh = jnp.pad(k_all.astype(jnp.bfloat16).transpose(0, 2, 1, 3), ((0, 0), (0, 0), (0, s_pad - s), (0, 0)))
    vh = jnp.pad(v_all.astype(jnp.bfloat16).transpose(0, 2, 1, 3), ((0, 0), (0, 0), (0, s_pad - s), (0, 0)))
    out = pl.pallas_call(
        functools.partial(_sb_attn_kernel, tq=tq, past=past),
        grid=(b, h, lq // tq),
        in_specs=[pl.BlockSpec((1, 1, tq, d), lambda bi, hi, qi: (bi, hi, qi, 0)),
                  pl.BlockSpec((1, 1, s_pad, d), lambda bi, hi, qi: (bi, hi, 0, 0)),
                  pl.BlockSpec((1, 1, s_pad, d), lambda bi, hi, qi: (bi, hi, 0, 0))],
        out_specs=pl.BlockSpec((1, 1, tq, d), lambda bi, hi, qi: (bi, hi, qi, 0)),
        out_shape=jax.ShapeDtypeStruct((b, h, lq, d), F32),
        compiler_params=pltpu.CompilerParams(dimension_semantics=("parallel", "parallel", "arbitrary")),
        name="sb_attention",
    )(qh, kh, vh)
    return out.transpose(0, 2, 1, 3)


def causal_conv(u, buf, w, bias):
    L = u.shape[1]
    full = jnp.concatenate([buf, u], axis=1)
    out = bias
    for i in range(SSD_CONV):
        out = out + full[:, i:i + L] * w[i]
    return out, full[:, -(SSD_CONV - 1):]


def ssd_scan(x, dt, a, bm, cm, h0):
    b, L, H, P = x.shape
    Q = SSD_CHUNK if L % SSD_CHUNK == 0 else L
    nc = L // Q
    rep = H // SSD_GROUPS
    xf = x.astype(F32).reshape(b, nc, Q, H, P)
    bh = jnp.repeat(bm.astype(F32), rep, axis=2).reshape(b, nc, Q, H, SSD_STATE)
    ch = jnp.repeat(cm.astype(F32), rep, axis=2).reshape(b, nc, Q, H, SSD_STATE)
    dtc = dt.reshape(b, nc, Q, H)
    a_cum = jnp.cumsum(dtc * a, axis=2)
    seg = a_cum[:, :, :, None, :] - a_cum[:, :, None, :, :]
    causal = jnp.tril(jnp.ones((Q, Q), dtype=bool))[None, None, :, :, None]
    decay = jnp.exp(jnp.where(causal, seg, -jnp.inf))
    cb = jnp.einsum('bclhn,bcshn->bclsh', ch, bh)
    y_diag = jnp.einsum('bclsh,bcshp->bclhp', cb * decay * dtc[:, :, None], xf)
    w_state = jnp.exp(a_cum[:, :, -1:] - a_cum) * dtc
    chunk_states = jnp.einsum('bclhn,bclh,bclhp->bchpn', bh, w_state, xf)
    chunk_decay = jnp.exp(a_cum[:, :, -1])

    def step(h, inp):
        dec, st = inp
        return dec[:, :, None, None] * h + st, h

    h_last, h_prev = lax.scan(step, h0, (chunk_decay.transpose(1, 0, 2),
                                          chunk_states.transpose(1, 0, 2, 3, 4)))
    h_prev = h_prev.transpose(1, 0, 2, 3, 4)
    y_off = jnp.einsum('bclhn,bchpn->bclhp', ch, h_prev) * jnp.exp(a_cum)[..., None]
    return (y_diag + y_off).reshape(b, L, H, P), h_last


def hybrid_mixer(xn, k_cache, v_cache, conv_buf, ssd_h,
                 w_in, conv_w, conv_b, dt_bias, a_log, d_skip, ssd_norm, w_out):
    b, L, _ = xn.shape
    proj = xn @ w_in
    q, k, v, z, xbc, dt = jnp.split(proj, HYB_SPLITS, axis=-1)
    q = q.reshape(b, L, SB_HEADS, SB_HEAD_DIM)
    k = k.reshape(b, L, SB_HEADS, SB_HEAD_DIM)
    v = v.reshape(b, L, SB_HEADS, SB_HEAD_DIM)
    past = k_cache.shape[1]
    k_all = jnp.concatenate([k_cache, k], axis=1)
    v_all = jnp.concatenate([v_cache, v], axis=1)
    o_sb = sb_attention(q, k_all, v_all, past).reshape(b, L, SB_WIDTH)
    xbc_c, new_buf = causal_conv(xbc, conv_buf, conv_w, conv_b)
    xbc_c = jax.nn.silu(xbc_c)
    xs, bm, cm = jnp.split(xbc_c, (SSD_INNER, SSD_INNER + SSD_GROUPS * SSD_STATE), axis=-1)
    xs = xs.reshape(b, L, SSD_HEADS, SSD_HEAD_DIM)
    bm = bm.reshape(b, L, SSD_GROUPS, SSD_STATE)
    cm = cm.reshape(b, L, SSD_GROUPS, SSD_STATE)
    dtp = jax.nn.softplus(dt.astype(F32) + dt_bias.astype(F32))
    a = -jnp.exp(a_log.astype(F32))
    y, h_new = ssd_scan(xs, dtp, a, bm, cm, ssd_h.astype(F32))
    y = y + d_skip.astype(F32)[:, None] * xs.astype(F32)
    y = y.reshape(b, L, SSD_INNER) * jax.nn.silu(z.astype(F32))
    yg = y.reshape(b, L, SSD_GROUPS, SSD_INNER // SSD_GROUPS)
    yg = yg * lax.rsqrt(jnp.mean(yg * yg, axis=-1, keepdims=True) + EPS)
    y = yg.reshape(b, L, SSD_INNER) * ssd_norm.astype(F32)
    out = jnp.concatenate([o_sb, y.astype(xn.dtype)], axis=-1) @ w_out
    return out, k, v, new_buf, h_new.astype(ssd_h.dtype)


def rwkv7_mixer(xn, shift_buf, wkv, mu, w_rkv, w0, w1, w2, a0, a1, a2, g1, g2,
                k_k, k_a, r_k, lnx_w, lnx_b, w_out):
    b, L, D = xn.shape
    x_prev = jnp.concatenate([shift_buf, xn[:, :-1]], axis=1)
    xx = x_prev - xn
    xm = xn[None] + xx[None] * mu[:, None, None, :]
    r, k, v = jnp.einsum('ibld,ide->ible', xm[:3], w_rkv).astype(F32)
    xw, xa, xg = xm[3], xm[4], xm[5]
    w = -jax.nn.softplus(-(w0 + jnp.tanh(xw @ w1) @ w2).astype(F32)) - 0.5
    decay = jnp.exp(-jnp.exp(w))
    a = jax.nn.sigmoid((a0 + (xa @ a1) @ a2).astype(F32))
    g = (jax.nn.sigmoid(xg @ g1) @ g2).astype(F32)
    hs = (b, L, RW_HEADS, RW_HEAD)
    kk = (k * k_k.astype(F32)).reshape(hs)
    kk = kk / jnp.maximum(jnp.sqrt(jnp.sum(kk * kk, axis=-1, keepdims=True)), 1e-12)
    k = k * (1.0 + (a - 1.0) * k_a.astype(F32))
    rh, dh, kh, vh, ah = (t.reshape(hs) for t in (r, decay, k, v, a))
    a_vec = -kk
    b_vec = kk * ah

    def step(S, inp):
        r_t, d_t, k_t, v_t, a_t, b_t = inp
        sa = jnp.einsum('bhvk,bhk->bhv', S, a_t)
        S = (S * d_t[:, :, None, :] + sa[..., None] * b_t[:, :, None, :]
             + v_t[..., None] * k_t[:, :, None, :])
        return S, jnp.einsum('bhvk,bhk->bhv', S, r_t)

    seqs = tuple(t.transpose(1, 0, 2, 3) for t in (rh, dh, kh, vh, a_vec, b_vec))
    S_last, o = lax.scan(step, wkv.astype(F32), seqs)
    o = o.transpose(1, 0, 2, 3)
    mean = jnp.mean(o, axis=-1, keepdims=True)
    var = jnp.mean(jnp.square(o - mean), axis=-1, keepdims=True)
    o = ((o - mean) * lax.rsqrt(var + RW_LNX_EPS)).reshape(b, L, D)
    o = o * lnx_w.astype(F32) + lnx_b.astype(F32)
    bonus = jnp.sum(rh * kh * r_k.astype(F32), axis=-1, keepdims=True) * vh
    o = o + bonus.reshape(b, L, D)
    out = (o * g).astype(xn.dtype) @ w_out
    return out, xn[:, -1:], S_last.astype(wkv.dtype)


PEER_TM = 256
PEER_TB = 32
PEER_HALF = D_MODEL // 2
PEER_SUB = PEER_HALF // 128
PEER_SEL = PEER_HEADS * PEER_TOPK
VMEM_LIMIT = 56 * 1024 * 1024


def _peer_scores_kernel(x_ref, g_ref, wq_ref, keys_ref, s_ref, xn_ref, xnb_ref):
    @pl.when(pl.program_id(1) == 0)
    def _():
        x = x_ref[...]
        xn = x * lax.rsqrt(jnp.mean(x * x, axis=-1, keepdims=True) + EPS) * g_ref[...]
        xn_ref[...] = xn
        xnb_ref[...] = xn.astype(jnp.bfloat16)

    q = jnp.dot(xnb_ref[...], wq_ref[...], preferred_element_type=F32).astype(jnp.bfloat16)
    half = PEER_QDIM // 2
    for i in range(2):
        s_ref[:, i * PEER_NKEYS:(i + 1) * PEER_NKEYS] = lax.dot_general(
            q[:, i * half:(i + 1) * half], keys_ref[0, i],
            (((1,), (1,)), ((), ())), preferred_element_type=F32)


def peer_scores(x, g, wq_bf, keys_bf):
    t, d = x.shape
    return pl.pallas_call(
        _peer_scores_kernel,
        grid=(t // PEER_TM, PEER_HEADS),
        in_specs=[pl.BlockSpec((PEER_TM, d), lambda i, h: (i, 0)),
                  pl.BlockSpec((1, d), lambda i, h: (0, 0)),
                  pl.BlockSpec((d, PEER_QDIM), lambda i, h: (0, h)),
                  pl.BlockSpec((1, 2, PEER_NKEYS, PEER_QDIM // 2), lambda i, h: (h, 0, 0, 0))],
        out_specs=[pl.BlockSpec((PEER_TM, 2 * PEER_NKEYS), lambda i, h: (i, h)),
                   pl.BlockSpec((PEER_TM, d), lambda i, h: (i, 0))],
        out_shape=[jax.ShapeDtypeStruct((t, PEER_HEADS * 2 * PEER_NKEYS), F32),
                   jax.ShapeDtypeStruct((t, d), F32)],
        scratch_shapes=[pltpu.VMEM((PEER_TM, d), jnp.bfloat16)],
        compiler_params=pltpu.CompilerParams(dimension_semantics=("parallel", "arbitrary")),
        name="peer_scores",
    )(x, g.reshape(1, d), wq_bf, keys_bf)


def pack_table(tab):
    e = tab.shape[0]
    b = lax.bitcast_convert_type(tab.astype(jnp.bfloat16), jnp.uint16).astype(jnp.uint32)
    b = b.reshape(e, 2, PEER_SUB, 128)
    return (b[:, 0] << 16) | b[:, 1]


def _unpack_row(w):
    hi = lax.bitcast_convert_type(w & jnp.uint32(0xFFFF0000), F32)
    lo = lax.bitcast_convert_type(w << 16, F32)
    return hi, lo


def _peer_hidden_kernel(eid_ref, x_ref, tab_ref, hid_ref, p_ref):
    def token(t, carry):
        xhi = x_ref[t, 0]
        xlo = x_ref[t, 1]
        for j in range(PEER_SEL):
            hi, lo = _unpack_row(tab_ref[eid_ref[t, j]])
            p = hi * xhi + lo * xlo
            p_ref[j:j + 1, :] = jnp.sum(p, axis=0, keepdims=True)
        hid_ref[t] = jnp.sum(p_ref[...].T, axis=0, keepdims=True)
        return carry

    lax.fori_loop(0, PEER_TB, token, 0)


def _peer_out_kernel(eid_ref, act_ref, res_ref, tab_ref, out_ref):
    nacc = 4

    def token(t, carry):
        acc_hi = [jnp.zeros((PEER_SUB, 128), F32) for _ in range(nacc)]
        acc_lo = [jnp.zeros((PEER_SUB, 128), F32) for _ in range(nacc)]
        for j in range(PEER_SEL):
            hi, lo = _unpack_row(tab_ref[eid_ref[t, j]])
            a = act_ref[t, j]
            acc_hi[j % nacc] = acc_hi[j % nacc] + a * hi
            acc_lo[j % nacc] = acc_lo[j % nacc] + a * lo
        out_ref[t, 0] = res_ref[t, 0] + ((acc_hi[0] + acc_hi[1]) + (acc_hi[2] + acc_hi[3]))
        out_ref[t, 1] = res_ref[t, 1] + ((acc_lo[0] + acc_lo[1]) + (acc_lo[2] + acc_lo[3]))
        return carry

    lax.fori_loop(0, PEER_TB, token, 0)


def _table_spec(n_exp):
    return pl.BlockSpec((n_exp, PEER_SUB, 128), lambda i: (0, 0, 0), pipeline_mode=pl.Buffered(1))


def peer_hidden(eid, xn_split, tab_packed):
    t = eid.shape[0]
    n_exp = tab_packed.shape[0]
    return pl.pallas_call(
        _peer_hidden_kernel,
        grid=(t // PEER_TB,),
        in_specs=[pl.BlockSpec((PEER_TB, PEER_SEL), lambda i: (i, 0), memory_space=pltpu.SMEM),
                  pl.BlockSpec((PEER_TB, 2, PEER_SUB, 128), lambda i: (i, 0, 0, 0)),
                  _table_spec(n_exp)],
        out_specs=pl.BlockSpec((PEER_TB, 1, PEER_SEL), lambda i: (i, 0, 0)),
        out_shape=jax.ShapeDtypeStruct((t, 1, PEER_SEL), F32),
        scratch_shapes=[pltpu.VMEM((PEER_SEL, 128), F32)],
        compiler_params=pltpu.CompilerParams(dimension_semantics=("parallel",),
                                             vmem_limit_bytes=VMEM_LIMIT),
        name="peer_hidden",
    )(eid, xn_split, tab_packed)


def peer_out(eid, act, res_split, tab_packed):
    t = eid.shape[0]
    n_exp = tab_packed.shape[0]
    return pl.pallas_call(
        _peer_out_kernel,
        grid=(t // PEER_TB,),
        in_specs=[pl.BlockSpec((PEER_TB, PEER_SEL), lambda i: (i, 0), memory_space=pltpu.SMEM),
                  pl.BlockSpec((PEER_TB, PEER_SEL), lambda i: (i, 0), memory_space=pltpu.SMEM),
                  pl.BlockSpec((PEER_TB, 2, PEER_SUB, 128), lambda i: (i, 0, 0, 0)),
                  _table_spec(n_exp)],
        out_specs=pl.BlockSpec((PEER_TB, 2, PEER_SUB, 128), lambda i: (i, 0, 0, 0)),
        out_shape=jax.ShapeDtypeStruct((t, 2, PEER_SUB, 128), F32),
        compiler_params=pltpu.CompilerParams(dimension_semantics=("parallel",),
                                             vmem_limit_bytes=VMEM_LIMIT),
        name="peer_out",
    )(eid, act, res_split, tab_packed)


def peer_layer(x, ng, w_q, keys, u_packed, v_packed):
    t, d = x.shape
    scores, xn = peer_scores(x, ng, w_q.astype(jnp.bfloat16), keys.astype(jnp.bfloat16))
    s = scores.reshape(t, PEER_HEADS, 2, PEER_NKEYS)
    s1, i1 = lax.top_k(s[:, :, 0], PEER_TOPK)
    s2, i2 = lax.top_k(s[:, :, 1], PEER_TOPK)
    cand = (s1[..., :, None] + s2[..., None, :]).reshape(t, PEER_HEADS, PEER_TOPK * PEER_TOPK)
    cand_id = (i1[..., :, None] * PEER_NKEYS + i2[..., None, :]).reshape(t, PEER_HEADS, PEER_TOPK * PEER_TOPK)
    top, pos = lax.top_k(cand, PEER_TOPK)
    eid = jnp.take_along_axis(cand_id, pos, axis=-1).reshape(t, PEER_SEL)
    gate = jax.nn.softmax(top, axis=-1).reshape(t, PEER_SEL)
    hidden = peer_hidden(eid, xn.reshape(t, 2, PEER_SUB, 128), u_packed).reshape(t, PEER_SEL)
    act = jax.nn.gelu(hidden, approximate=False) * gate
    out = peer_out(eid, act, x.reshape(t, 2, PEER_SUB, 128), v_packed)
    return out.reshape(t, d)


def mixer_layer(l, x, states, mp):
    xn = rmsnorm(x, mp[0])
    if l % 2 == 0:
        out, *ns = hybrid_mixer(xn, *states, *mp[1:])
    else:
        out, *ns = rwkv7_mixer(xn, *states, *mp[1:])
    return x + out, ns


def run_trunks(xs, states_list, layer_params, peer_params, final_norm):
    new_states = [[] for _ in xs]
    sizes = [x.shape[0] * x.shape[1] for x in xs]
    offs = np.cumsum([0] + sizes)
    for l in range(DEPTH):
        mixed = []
        for gi, x in enumerate(xs):
            x, ns = mixer_layer(l, x, states_list[gi][l], layer_params[l])
            new_states[gi].extend(ns)
            mixed.append(x.reshape(-1, D_MODEL))
        ng, wq, keys, u_packed, v_packed = peer_params[l]
        y = peer_layer(jnp.concatenate(mixed, axis=0), ng, wq, keys, u_packed, v_packed)
        xs = [y[offs[gi]:offs[gi + 1]].reshape(xs[gi].shape) for gi in range(len(xs))]
    ys = [pallas_rmsnorm(x, final_norm) for x in xs]
    return ys, new_states


def kernel(x_prompt, x_sample, cache_sb_k, cache_sb_v, state_ssd_conv, state_ssd,
           state_rwkv_shift, state_rwkv_wkv,
           l0_norm_mix, l0_w_in, l0_conv_w, l0_conv_b, l0_dt_bias, l0_a_log, l0_d_skip,
           l0_ssd_norm, l0_w_out,
           l1_norm_mix, l1_mu, l1_w_rkv, l1_w0, l1_w1, l1_w2, l1_a0, l1_a1, l1_a2,
           l1_g1, l1_g2, l1_k_k, l1_k_a, l1_r_k, l1_lnx_w, l1_lnx_b, l1_w_out,
           l0_norm_ffn, l0_peer_wq, l0_peer_keys, l0_peer_u, l0_peer_v,
           l1_norm_ffn, l1_peer_wq, l1_peer_keys, l1_peer_u, l1_peer_v,
           final_norm):
    layer_params = [
        (l0_norm_mix, l0_w_in, l0_conv_w, l0_conv_b, l0_dt_bias, l0_a_log, l0_d_skip,
         l0_ssd_norm, l0_w_out),
        (l1_norm_mix, l1_mu, l1_w_rkv, l1_w0, l1_w1, l1_w2, l1_a0, l1_a1, l1_a2,
         l1_g1, l1_g2, l1_k_k, l1_k_a, l1_r_k, l1_lnx_w, l1_lnx_b, l1_w_out),
    ]
    peer_params = [
        (l0_norm_ffn, l0_peer_wq, l0_peer_keys, pack_table(l0_peer_u), pack_table(l0_peer_v)),
        (l1_norm_ffn, l1_peer_wq, l1_peer_keys, pack_table(l1_peer_u), pack_table(l1_peer_v)),
    ]
    bp = x_prompt.shape[0]
    dt = x_prompt.dtype
    zero_states = [
        (jnp.zeros((bp, 0, SB_HEADS, SB_HEAD_DIM), dt), jnp.zeros((bp, 0, SB_HEADS, SB_HEAD_DIM), dt),
         jnp.zeros((bp, SSD_CONV - 1, SSD_CONV_DIM), dt),
         jnp.zeros((bp, SSD_HEADS, SSD_HEAD_DIM, SSD_STATE), dt)),
        (jnp.zeros((bp, 1, D_MODEL), dt), jnp.zeros((bp, RW_HEADS, RW_HEAD, RW_HEAD), dt)),
    ]
    sample_states = [
        (cache_sb_k, cache_sb_v, state_ssd_conv, state_ssd),
        (state_rwkv_shift, state_rwkv_wkv),
    ]
    (y_prompt, y_sample), (p_states, s_states) = run_trunks(
        [x_prompt, x_sample], [zero_states, sample_states], layer_params, peer_params, final_norm)
    p_sb_k, p_sb_v, p_conv, p_ssd, p_shift, p_wkv = p_states
    s_sb_k, s_sb_v, s_conv, s_ssd, s_shift, s_wkv = s_states
    return (y_prompt, y_sample, p_sb_k, p_sb_v, p_conv, p_ssd, p_shift, p_wkv,
            s_sb_k, s_sb_v, s_conv, s_ssd, s_shift, s_wkv)
```

```python
import functools
import math
import jax, jax.numpy as jnp
from jax import lax
import numpy as np
from jax.experimental import pallas as pl
from jax.experimental.pallas import tpu as pltpu

D_MODEL = 1024
DEPTH = 2

CHUNK = 64
EPS = 1e-6
SB_HEADS = 8
SB_HEAD_DIM = 64
SB_WIDTH = SB_HEADS * SB_HEAD_DIM
SB_BLOCK = 128
SSD_HEADS = 8
SSD_HEAD_DIM = 64
SSD_INNER = SSD_HEADS * SSD_HEAD_DIM
SSD_GROUPS = 2
SSD_STATE = 64
SSD_CONV = 4
SSD_CONV_DIM = SSD_INNER + 2 * SSD_GROUPS * SSD_STATE
SSD_CHUNK = CHUNK
HYB_SPLITS = (SB_WIDTH, 2 * SB_WIDTH, 3 * SB_WIDTH, 3 * SB_WIDTH + SSD_INNER,
              3 * SB_WIDTH + SSD_INNER + SSD_CONV_DIM)
HYB_IN = 3 * SB_WIDTH + SSD_INNER + SSD_CONV_DIM + SSD_HEADS
RW_HEAD = 64
RW_HEADS = D_MODEL // RW_HEAD
RW_LNX_EPS = 64e-5
PEER_HEADS = 8
PEER_NKEYS = 128
PEER_QDIM = 256
PEER_TOPK = 16
PEER_BLOCK = 128

F32 = jnp.float32
V7X_VMEM_BYTES = 64 * 1024 * 1024


def _cparams(*semantics):
    return pltpu.CompilerParams(dimension_semantics=semantics, vmem_limit_bytes=V7X_VMEM_BYTES)


def rmsnorm(x, g):
    xf = x.astype(F32)
    y = xf * lax.rsqrt(jnp.mean(xf * xf, axis=-1, keepdims=True) + EPS)
    return (y * g.astype(F32)).astype(x.dtype)


def _rmsnorm_kernel(x_ref, g_ref, o_ref):
    x = x_ref[...]
    y = x * lax.rsqrt(jnp.mean(x * x, axis=-1, keepdims=True) + EPS)
    o_ref[...] = y * g_ref[...]


def pallas_rmsnorm(x, g, rows=256):
    shp = x.shape
    x2 = x.reshape(-1, shp[-1])
    t, d = x2.shape
    rows = min(rows, t)
    out = pl.pallas_call(
        _rmsnorm_kernel,
        grid=(t // rows,),
        in_specs=[pl.BlockSpec((rows, d), lambda i: (i, 0)),
                  pl.BlockSpec((1, d), lambda i: (0, 0))],
        out_specs=pl.BlockSpec((rows, d), lambda i: (i, 0)),
        out_shape=jax.ShapeDtypeStruct((t, d), x.dtype),
        compiler_params=_cparams("parallel"),
        name="final_rmsnorm",
    )(x2, g.reshape(1, d))
    return out.reshape(shp)


SB_KBLOCK = 128
SB_LOG_FLOOR = -90.0


def _split3(x):
    h = x.astype(jnp.bfloat16)
    r = x - h.astype(F32)
    m = r.astype(jnp.bfloat16)
    l = (r - m.astype(F32)).astype(jnp.bfloat16)
    return h, m, l


def _sb_attn_kernel(q_ref, k_ref, v_ref, o_ref, *, tq, past):
    iq = pl.program_id(2)
    q = q_ref[0, 0]
    q_pos = past + iq * tq + lax.broadcasted_iota(jnp.int32, (tq, SB_KBLOCK), 0)
    k_off = lax.broadcasted_iota(jnp.int32, (tq, SB_KBLOCK), 1)
    r = lax.broadcasted_iota(jnp.int32, (SB_KBLOCK, SB_KBLOCK), 0)
    c = lax.broadcasted_iota(jnp.int32, (SB_KBLOCK, SB_KBLOCK), 1)
    upper = (r > c).astype(jnp.bfloat16)
    j0 = (past + (iq + 1) * tq - 1) // SB_KBLOCK

    def cond(state):
        j, live, _, _ = state
        return jnp.logical_and(j >= 0, live > 0)

    def body(state):
        j, _, carry, acc = state
        start = pl.multiple_of(j * SB_KBLOCK, SB_KBLOCK)
        kb = k_ref[0, 0, pl.ds(start, SB_KBLOCK), :]
        vb = v_ref[0, 0, pl.ds(start, SB_KBLOCK), :]
        z = lax.dot_general(q, kb, (((1,), (1,)), ((), ())), preferred_element_type=F32)
        mask = (j * SB_KBLOCK + k_off) < q_pos
        sp = jnp.maximum(z, 0.0) + jnp.log1p(jnp.exp(-jnp.abs(z)))
        log_fail = jnp.where(mask, -sp, 0.0)
        h, m, l = _split3(log_fail)
        later = (jnp.dot(h, upper, preferred_element_type=F32)
                 + jnp.dot(m, upper, preferred_element_type=F32)
                 + jnp.dot(l, upper, preferred_element_type=F32))
        log_w = (z - sp) + later + carry
        w = jnp.where(mask, jnp.exp(log_w), 0.0)
        acc = acc + jnp.dot(w.astype(jnp.bfloat16), vb, preferred_element_type=F32)
        carry = carry + jnp.sum(log_fail, axis=-1, keepdims=True)
        live = (jnp.max(carry) > SB_LOG_FLOOR).astype(jnp.int32)
        return j - 1, live, carry, acc

    state = (j0, jnp.int32(1), jnp.zeros((tq, 1), F32), jnp.zeros((tq, SB_HEAD_DIM), F32))
    _, _, _, acc = lax.while_loop(cond, body, state)
    o_ref[0, 0] = acc


def sb_attention(q, k_all, v_all, past):
    b, lq, h, d = q.shape
    s = k_all.shape[1]
    tq = min(lq, 128)
    s_pad = -(-s // SB_KBLOCK) * SB_KBLOCK
    qh = (q * (d ** -0.5)).astype(jnp.bfloat16).transpose(0, 2, 1, 3)
    kh = jnp.pad(k_all.astype(jnp.bfloat16).transpose(0, 2, 1, 3), ((0, 0), (0, 0), (0, s_pad - s), (0, 0)))
    vh = jnp.pad(v_all.astype(jnp.bfloat16).transpose(0, 2, 1, 3), ((0, 0), (0, 0), (0, s_pad - s), (0, 0)))
    out = pl.pallas_call(
        functools.partial(_sb_attn_kernel, tq=tq, past=past),
        grid=(b, h, lq // tq),
        in_specs=[pl.BlockSpec((1, 1, tq, d), lambda bi, hi, qi: (bi, hi, qi, 0)),
                  pl.BlockSpec((1, 1, s_pad, d), lambda bi, hi, qi: (bi, hi, 0, 0)),
                  pl.BlockSpec((1, 1, s_pad, d), lambda bi, hi, qi: (bi, hi, 0, 0))],
        out_specs=pl.BlockSpec((1, 1, tq, d), lambda bi, hi, qi: (bi, hi, qi, 0)),
        out_shape=jax.ShapeDtypeStruct((b, h, lq, d), F32),
        compiler_params=_cparams("parallel", "parallel", "arbitrary"),
        name="sb_attention",
    )(qh, kh, vh)
    return out.transpose(0, 2, 1, 3)


def causal_conv(u, buf, w, bias):
    L = u.shape[1]
    full = jnp.concatenate([buf, u], axis=1)
    out = bias
    for i in range(SSD_CONV):
        out = out + full[:, i:i + L] * w[i]
    return out, full[:, -(SSD_CONV - 1):]


def ssd_scan(x, dt, a, bm, cm, h0):
    b, L, H, P = x.shape
    Q = SSD_CHUNK if L % SSD_CHUNK == 0 else L
    nc = L // Q
    rep = H // SSD_GROUPS
    xf = x.astype(F32).reshape(b, nc, Q, H, P)
    bh = jnp.repeat(bm.astype(F32), rep, axis=2).reshape(b, nc, Q, H, SSD_STATE)
    ch = jnp.repeat(cm.astype(F32), rep, axis=2).reshape(b, nc, Q, H, SSD_STATE)
    dtc = dt.reshape(b, nc, Q, H)
    a_cum = jnp.cumsum(dtc * a, axis=2)
    seg = a_cum[:, :, :, None, :] - a_cum[:, :, None, :, :]
    causal = jnp.tril(jnp.ones((Q, Q), dtype=bool))[None, None, :, :, None]
    decay = jnp.exp(jnp.where(causal, seg, -jnp.inf))
    cb = jnp.einsum('bclhn,bcshn->bclsh', ch, bh)
    y_diag = jnp.einsum('bclsh,bcshp->bclhp', cb * decay * dtc[:, :, None], xf)
    w_state = jnp.exp(a_cum[:, :, -1:] - a_cum) * dtc
    chunk_states = jnp.einsum('bclhn,bclh,bclhp->bchpn', bh, w_state, xf)
    chunk_decay = jnp.exp(a_cum[:, :, -1])

    def step(h, inp):
        dec, st = inp
        return dec[:, :, None, None] * h + st, h

    h_last, h_prev = lax.scan(step, h0, (chunk_decay.transpose(1, 0, 2),
                                          chunk_states.transpose(1, 0, 2, 3, 4)))
    h_prev = h_prev.transpose(1, 0, 2, 3, 4)
    y_off = jnp.einsum('bclhn,bchpn->bclhp', ch, h_prev) * jnp.exp(a_cum)[..., None]
    return (y_diag + y_off).reshape(b, L, H, P), h_last


def hybrid_mixer(xn, k_cache, v_cache, conv_buf, ssd_h,
                 w_in, conv_w, conv_b, dt_bias, a_log, d_skip, ssd_norm, w_out):
    b, L, _ = xn.shape
    proj = xn @ w_in
    q, k, v, z, xbc, dt = jnp.split(proj, HYB_SPLITS, axis=-1)
    q = q.reshape(b, L, SB_HEADS, SB_HEAD_DIM)
    k = k.reshape(b, L, SB_HEADS, SB_HEAD_DIM)
    v = v.reshape(b, L, SB_HEADS, SB_HEAD_DIM)
    past = k_cache.shape[1]
    k_all = jnp.concatenate([k_cache, k], axis=1)
    v_all = jnp.concatenate([v_cache, v], axis=1)
    o_sb = sb_attention(q, k_all, v_all, past).reshape(b, L, SB_WIDTH)
    xbc_c, new_buf = causal_conv(xbc, conv_buf, conv_w, conv_b)
    xbc_c = jax.nn.silu(xbc_c)
    xs, bm, cm = jnp.split(xbc_c, (SSD_INNER, SSD_INNER + SSD_GROUPS * SSD_STATE), axis=-1)
    xs = xs.reshape(b, L, SSD_HEADS, SSD_HEAD_DIM)
    bm = bm.reshape(b, L, SSD_GROUPS, SSD_STATE)
    cm = cm.reshape(b, L, SSD_GROUPS, SSD_STATE)
    dtp = jax.nn.softplus(dt.astype(F32) + dt_bias.astype(F32))
    a = -jnp.exp(a_log.astype(F32))
    y, h_new = ssd_scan(xs, dtp, a, bm, cm, ssd_h.astype(F32))
    y = y + d_skip.astype(F32)[:, None] * xs.astype(F32)
    y = y.reshape(b, L, SSD_INNER) * jax.nn.silu(z.astype(F32))
    yg = y.reshape(b, L, SSD_GROUPS, SSD_INNER // SSD_GROUPS)
    yg = yg * lax.rsqrt(jnp.mean(yg * yg, axis=-1, keepdims=True) + EPS)
    y = yg.reshape(b, L, SSD_INNER) * ssd_norm.astype(F32)
    out = jnp.concatenate([o_sb, y.astype(xn.dtype)], axis=-1) @ w_out
    return out, k, v, new_buf, h_new.astype(ssd_h.dtype)


RW_PAIRS = RW_HEADS // 2
RW_NP = 4
RW_TC = 64


def _rwkv_scan_kernel(r_ref, d_ref, k_ref, v_ref, a_ref, b_ref, s0_ref, o_ref, st_ref, s_scr, vcol_scr, *, tc):
    c = pl.program_id(2)

    @pl.when(c == 0)
    def _():
        s_scr[...] = s0_ref[0]

    row = lax.broadcasted_iota(jnp.int32, (RW_HEAD, 128), 0)
    lane = lax.broadcasted_iota(jnp.int32, (RW_HEAD, 128), 1)
    left = lane < RW_HEAD
    diag = (lane & (RW_HEAD - 1)) == row
    same_head = ((lax.broadcasted_iota(jnp.int32, (128, 128), 0) < RW_HEAD)
                 == (lax.broadcasted_iota(jnp.int32, (128, 128), 1) < RW_HEAD)).astype(jnp.bfloat16)

    def half_sums(x):
        lo = jnp.sum(jnp.where(left, x, 0.0), axis=1, keepdims=True)
        hi = jnp.sum(jnp.where(left, 0.0, x), axis=1, keepdims=True)
        return jnp.where(left, lo, hi)

    def vcol_body(t8, carry):
        base = pl.multiple_of(t8 * 8, 8)
        for p in range(RW_NP):
            vblk = v_ref[0, pl.ds(base, 8), p * 128:(p + 1) * 128]
            for i in range(8):
                h, m, l = _split3(jnp.where(diag, vblk[i:i + 1], 0.0))
                vcol_scr[p, base + i] = (jnp.dot(h, same_head, preferred_element_type=F32)
                                         + jnp.dot(m, same_head, preferred_element_type=F32)
                                         + jnp.dot(l, same_head, preferred_element_type=F32))
        return carry

    lax.fori_loop(0, tc // 8, vcol_body, 0)

    def step8(t8, carry):
        base = pl.multiple_of(t8 * 8, 8)
        for p in range(RW_NP):
            sl = slice(p * 128, (p + 1) * 128)
            a8 = a_ref[0, pl.ds(base, 8), sl]
            d8 = d_ref[0, pl.ds(base, 8), sl]
            b8 = b_ref[0, pl.ds(base, 8), sl]
            k8 = k_ref[0, pl.ds(base, 8), sl]
            r8 = r_ref[0, pl.ds(base, 8), sl]
            s = s_scr[p]
            rows = []
            for i in range(8):
                sa = half_sums(s * a8[i:i + 1])
                s = s * d8[i:i + 1] + sa * b8[i:i + 1] + vcol_scr[p, base + i] * k8[i:i + 1]
                oc = half_sums(s * r8[i:i + 1])
                rows.append(jnp.sum(jnp.where(diag, oc, 0.0), axis=0, keepdims=True))
            s_scr[p] = s
            o_ref[0, pl.ds(base, 8), sl] = jnp.concatenate(rows, axis=0)
        return carry

    lax.fori_loop(0, tc // 8, step8, 0)

    @pl.when(c == pl.num_programs(2) - 1)
    def _():
        st_ref[0] = s_scr[...]


def rwkv_scan(r, d, k, v, a, b, wkv):
    bsz, L, D = r.shape
    tc = min(RW_TC, L)
    s0 = wkv.reshape(bsz, RW_PAIRS, 2, RW_HEAD, RW_HEAD).transpose(0, 1, 3, 2, 4).reshape(bsz, RW_PAIRS, RW_HEAD, 128)
    seq = pl.BlockSpec((1, tc, RW_NP * 128), lambda bi, g, c: (bi, c, g))
    st = pl.BlockSpec((1, RW_NP, RW_HEAD, 128), lambda bi, g, c: (bi, g, 0, 0))
    o, s_new = pl.pallas_call(
        functools.partial(_rwkv_scan_kernel, tc=tc),
        grid=(bsz, RW_PAIRS // RW_NP, L // tc),
        in_specs=[seq] * 6 + [st],
        out_specs=[seq, st],
        out_shape=[jax.ShapeDtypeStruct((bsz, L, D), F32),
                   jax.ShapeDtypeStruct((bsz, RW_PAIRS, RW_HEAD, 128), F32)],
        scratch_shapes=[pltpu.VMEM((RW_NP, RW_HEAD, 128), F32),
                        pltpu.VMEM((RW_NP, tc, RW_HEAD, 128), F32)],
        compiler_params=_cparams("parallel", "parallel", "arbitrary"),
        name="rwkv_scan",
    )(r, d, k, v, a, b, s0)
    s_new = s_new.reshape(bsz, RW_PAIRS, RW_HEAD, 2, RW_HEAD).transpose(0, 1, 3, 2, 4).reshape(bsz, RW_HEADS, RW_HEAD, RW_HEAD)
    return o, s_new


def _proj3_kernel(x_ref, w_ref, o_ref):
    o_ref[0] = jnp.dot(x_ref[0], w_ref[0], preferred_element_type=F32)


def proj3(x3, w3):
    _, m, kdim = x3.shape
    n = w3.shape[2]
    tm = min(m, 512)
    tn = min(n, 512)
    return pl.pallas_call(
        _proj3_kernel,
        grid=(3, m // tm, n // tn),
        in_specs=[pl.BlockSpec((1, tm, kdim), lambda i, a, b: (i, a, 0)),
                  pl.BlockSpec((1, kdim, tn), lambda i, a, b: (i, 0, b))],
        out_specs=pl.BlockSpec((1, tm, tn), lambda i, a, b: (i, a, b)),
        out_shape=jax.ShapeDtypeStruct((3, m, n), F32),
        compiler_params=_cparams("parallel", "parallel", "parallel"),
        name="rwkv_proj3",
    )(x3, w3)


def rwkv7_mixer(xn, shift_buf, wkv, mu, w_rkv, w0, w1, w2, a0, a1, a2, g1, g2,
                k_k, k_a, r_k, lnx_w, lnx_b, w_out):
    b, L, D = xn.shape
    x_prev = jnp.concatenate([shift_buf, xn[:, :-1]], axis=1)
    xx = x_prev - xn
    xm = xn[None] + xx[None] * mu[:, None, None, :]
    r, k, v = proj3(xm[:3].astype(jnp.bfloat16).reshape(3, b * L, D),
                    w_rkv.astype(jnp.bfloat16)).reshape(3, b, L, D)
    xw, xa, xg = xm[3], xm[4], xm[5]
    w = -jax.nn.softplus(-(w0 + jnp.tanh(xw @ w1) @ w2).astype(F32)) - 0.5
    decay = jnp.exp(-jnp.exp(w))
    a = jax.nn.sigmoid((a0 + (xa @ a1) @ a2).astype(F32))
    g = (jax.nn.sigmoid(xg @ g1) @ g2).astype(F32)
    hs = (b, L, RW_HEADS, RW_HEAD)
    kk = (k * k_k.astype(F32)).reshape(hs)
    kk = kk / jnp.maximum(jnp.sqrt(jnp.sum(kk * kk, axis=-1, keepdims=True)), 1e-12)
    k = k * (1.0 + (a - 1.0) * k_a.astype(F32))
    rh, dh, kh, vh, ah = (t.reshape(hs) for t in (r, decay, k, v, a))
    a_vec = -kk
    b_vec = kk * ah
    o, S_last = rwkv_scan(r, decay, k, v, a_vec.reshape(b, L, D), b_vec.reshape(b, L, D), wkv.astype(F32))
    o = o.reshape(hs)
    mean = jnp.mean(o, axis=-1, keepdims=True)
    var = jnp.mean(jnp.square(o - mean), axis=-1, keepdims=True)
    o = ((o - mean) * lax.rsqrt(var + RW_LNX_EPS)).reshape(b, L, D)
    o = o * lnx_w.astype(F32) + lnx_b.astype(F32)
    bonus = jnp.sum(rh * kh * r_k.astype(F32), axis=-1, keepdims=True) * vh
    o = o + bonus.reshape(b, L, D)
    out = (o * g).astype(xn.dtype) @ w_out
    return out, xn[:, -1:], S_last.astype(wkv.dtype)


PEER_TM = 256
PEER_TB = 32
PEER_HALF = D_MODEL // 2
PEER_SUB = PEER_HALF // 128
PEER_SEL = PEER_HEADS * PEER_TOPK


def _peer_scores_kernel(x_ref, g_ref, wq_ref, keys_ref, s_ref, xn_ref, xnb_ref):
    @pl.when(pl.program_id(1) == 0)
    def _():
        x = x_ref[...]
        xn = x * lax.rsqrt(jnp.mean(x * x, axis=-1, keepdims=True) + EPS) * g_ref[...]
        xn_ref[...] = xn
        xnb_ref[...] = xn.astype(jnp.bfloat16)

    q = jnp.dot(xnb_ref[...], wq_ref[...], preferred_element_type=F32).astype(jnp.bfloat16)
    half = PEER_QDIM // 2
    for i in range(2):
        s_ref[i * PEER_NKEYS:(i + 1) * PEER_NKEYS, :] = lax.dot_general(
            keys_ref[0, i], q[:, i * half:(i + 1) * half],
            (((1,), (1,)), ((), ())), preferred_element_type=F32)


def peer_scores(x, g, wq_bf, keys_bf):
    t, d = x.shape
    return pl.pallas_call(
        _peer_scores_kernel,
        grid=(t // PEER_TM, PEER_HEADS),
        in_specs=[pl.BlockSpec((PEER_TM, d), lambda i, h: (i, 0)),
                  pl.BlockSpec((1, d), lambda i, h: (0, 0)),
                  pl.BlockSpec((d, PEER_QDIM), lambda i, h: (0, h)),
                  pl.BlockSpec((1, 2, PEER_NKEYS, PEER_QDIM // 2), lambda i, h: (h, 0, 0, 0))],
        out_specs=[pl.BlockSpec((2 * PEER_NKEYS, PEER_TM), lambda i, h: (h, i)),
                   pl.BlockSpec((PEER_TM, d), lambda i, h: (i, 0))],
        out_shape=[jax.ShapeDtypeStruct((PEER_HEADS * 2 * PEER_NKEYS, t), F32),
                   jax.ShapeDtypeStruct((t, d), F32)],
        scratch_shapes=[pltpu.VMEM((PEER_TM, d), jnp.bfloat16)],
        compiler_params=_cparams("parallel", "arbitrary"),
        name="peer_scores",
    )(x, g.reshape(1, d), wq_bf, keys_bf)


PEER_TT = 128
NEG_INF = float("-inf")


def _peer_candidates():
    k = PEER_TOPK
    a_idx, b_idx, valid = [], [], []
    for a in range(k // 2):
        nb = k // (a + 1)
        rows = k if a == 0 else 8
        for b in range(rows):
            a_idx.append(a); b_idx.append(b); valid.append(b < nb)
    for a in range(k // 2, k):
        a_idx.append(a); b_idx.append(0); valid.append(True)
    return np.array(a_idx), np.array(b_idx), np.array(valid)


_CAND_A, _CAND_B, _CAND_VALID = _peer_candidates()
PEER_NCAND = len(_CAND_A)


def _top_rows(x, tag, k):
    big = jnp.float32(1e9)
    vals, tags = [], []
    for _ in range(k):
        m = jnp.max(x, axis=0, keepdims=True)
        t = jnp.min(jnp.where(x == m, tag, big), axis=0, keepdims=True)
        vals.append(m)
        tags.append(t)
        x = jnp.where(tag == t, NEG_INF, x)
    return vals, tags


def _peer_topk_kernel(s_ref, pos_ref, eid_ref, gate_ref, v_scr, i_scr, e_scr, g_scr):
    k = PEER_TOPK
    key_tag = lax.broadcasted_iota(jnp.int32, (PEER_NKEYS, PEER_TT), 0).astype(F32)

    def half_body(g, carry):
        start = pl.multiple_of(g * PEER_NKEYS, PEER_NKEYS)
        vals, tags = _top_rows(s_ref[pl.ds(start, PEER_NKEYS), :], key_tag, k)
        v_scr[g] = jnp.concatenate(vals, axis=0)
        i_scr[g] = jnp.concatenate(tags, axis=0)
        return carry

    lax.fori_loop(0, 2 * PEER_HEADS, half_body, 0)

    pos = pos_ref[...]
    valid = pos >= 0.0

    def head_body(h, carry):
        s1 = v_scr[2 * h]; s2 = v_scr[2 * h + 1]
        i1 = i_scr[2 * h]; i2 = i_scr[2 * h + 1]
        e2 = i2[0:8]
        pieces_s = [s1[0:1] + s2]
        pieces_e = [i1[0:1] * PEER_NKEYS + i2]
        for a in range(1, k // 2):
            pieces_s.append(s1[a:a + 1] + s2[0:8])
            pieces_e.append(i1[a:a + 1] * PEER_NKEYS + e2)
        pieces_s.append(s1[k // 2:k] + s2[0:1])
        pieces_e.append(i1[k // 2:k] * PEER_NKEYS + i2[0:1])
        cand = jnp.where(valid, jnp.concatenate(pieces_s, axis=0), NEG_INF)
        cid = jnp.concatenate(pieces_e, axis=0)
        tops, tags = _top_rows(cand, pos, k)
        eids = [jnp.max(jnp.where(pos == t, cid, -1.0), axis=0, keepdims=True) for t in tags]
        top = jnp.concatenate(tops, axis=0)
        p = jnp.exp(top - top[0:1])
        gate = p / jnp.sum(p, axis=0, keepdims=True)
        row = pl.multiple_of(h * k, k)
        e_scr[pl.ds(row, k), :] = jnp.concatenate(eids, axis=0)
        g_scr[pl.ds(row, k), :] = gate
        return carry

    lax.fori_loop(0, PEER_HEADS, head_body, 0)
    eid_ref[...] = e_scr[...].T
    gate_ref[...] = g_scr[...].T


def peer_topk(scores_t):
    t = scores_t.shape[1]
    pos = np.where(_CAND_VALID, _CAND_A * PEER_TOPK + _CAND_B, -1).astype(np.float32)
    pos = jnp.asarray(np.broadcast_to(pos[:, None], (PEER_NCAND, PEER_TT)))
    return pl.pallas_call(
        _peer_topk_kernel,
        grid=(t // PEER_TT,),
        in_specs=[pl.BlockSpec((2 * PEER_HEADS * PEER_NKEYS, PEER_TT), lambda i: (0, i)),
                  pl.BlockSpec((PEER_NCAND, PEER_TT), lambda i: (0, 0))],
        out_specs=[pl.BlockSpec((PEER_TT, PEER_SEL), lambda i: (i, 0)),
                   pl.BlockSpec((PEER_TT, PEER_SEL), lambda i: (i, 0))],
        out_shape=[jax.ShapeDtypeStruct((t, PEER_SEL), F32),
                   jax.ShapeDtypeStruct((t, PEER_SEL), F32)],
        scratch_shapes=[pltpu.VMEM((2 * PEER_HEADS, PEER_TOPK, PEER_TT), F32),
                        pltpu.VMEM((2 * PEER_HEADS, PEER_TOPK, PEER_TT), F32),
                        pltpu.VMEM((PEER_SEL, PEER_TT), F32),
                        pltpu.VMEM((PEER_SEL, PEER_TT), F32)],
        compiler_params=_cparams("parallel"),
        name="peer_topk",
    )(scores_t, pos)


def pack_table(tab):
    e = tab.shape[0]
    b = lax.bitcast_convert_type(tab.astype(jnp.bfloat16), jnp.uint16).astype(jnp.uint32)
    b = b.reshape(e, 2, PEER_SUB, 128)
    return (b[:, 0] << 16) | b[:, 1]


def _unpack_row(w):
    hi = lax.bitcast_convert_type(w & jnp.uint32(0xFFFF0000), F32)
    lo = lax.bitcast_convert_type(w << 16, F32)
    return hi, lo


def _peer_hidden_kernel(eid_ref, x_ref, tab_ref, hid_ref, p_ref):
    def token(t, carry):
        xhi = x_ref[t, 0]
        xlo = x_ref[t, 1]
        for j in range(PEER_SEL):
            hi, lo = _unpack_row(tab_ref[eid_ref[t, j]])
            p = hi * xhi + lo * xlo
            p_ref[j:j + 1, :] = jnp.sum(p, axis=0, keepdims=True)
        hid_ref[t] = jnp.sum(p_ref[...].T, axis=0, keepdims=True)
        return carry

    lax.fori_loop(0, PEER_TB, token, 0)


def _peer_out_kernel(eid_ref, act_ref, res_ref, tab_ref, out_ref):
    nacc = 4

    def token(t, carry):
        acc_hi = [jnp.zeros((PEER_SUB, 128), F32) for _ in range(nacc)]
        acc_lo = [jnp.zeros((PEER_SUB, 128), F32) for _ in range(nacc)]
        for j in range(PEER_SEL):
            hi, lo = _unpack_row(tab_ref[eid_ref[t, j]])
            a = act_ref[t, j]
            acc_hi[j % nacc] = acc_hi[j % nacc] + a * hi
            acc_lo[j % nacc] = acc_lo[j % nacc] + a * lo
        out_ref[t, 0] = res_ref[t, 0] + ((acc_hi[0] + acc_hi[1]) + (acc_hi[2] + acc_hi[3]))
        out_ref[t, 1] = res_ref[t, 1] + ((acc_lo[0] + acc_lo[1]) + (acc_lo[2] + acc_lo[3]))
        return carry

    lax.fori_loop(0, PEER_TB, token, 0)


def _table_spec(n_exp):
    return pl.BlockSpec((n_exp, PEER_SUB, 128), lambda i: (0, 0, 0), pipeline_mode=pl.Buffered(1))


def peer_hidden(eid, xn_split, tab_packed):
    t = eid.shape[0]
    n_exp = tab_packed.shape[0]
    return pl.pallas_call(
        _peer_hidden_kernel,
        grid=(t // PEER_TB,),
        in_specs=[pl.BlockSpec((PEER_TB, PEER_SEL), lambda i: (i, 0), memory_space=pltpu.SMEM),
                  pl.BlockSpec((PEER_TB, 2, PEER_SUB, 128), lambda i: (i, 0, 0, 0)),
                  _table_spec(n_exp)],
        out_specs=pl.BlockSpec((PEER_TB, 1, PEER_SEL), lambda i: (i, 0, 0)),
        out_shape=jax.ShapeDtypeStruct((t, 1, PEER_SEL), F32),
        scratch_shapes=[pltpu.VMEM((PEER_SEL, 128), F32)],
        compiler_params=_cparams("parallel"),
        name="peer_hidden",
    )(eid, xn_split, tab_packed)


def peer_out(eid, act, res_split, tab_packed):
    t = eid.shape[0]
    n_exp = tab_packed.shape[0]
    return pl.pallas_call(
        _peer_out_kernel,
        grid=(t // PEER_TB,),
        in_specs=[pl.BlockSpec((PEER_TB, PEER_SEL), lambda i: (i, 0), memory_space=pltpu.SMEM),
                  pl.BlockSpec((PEER_TB, PEER_SEL), lambda i: (i, 0), memory_space=pltpu.SMEM),
                  pl.BlockSpec((PEER_TB, 2, PEER_SUB, 128), lambda i: (i, 0, 0, 0)),
                  _table_spec(n_exp)],
        out_specs=pl.BlockSpec((PEER_TB, 2, PEER_SUB, 128), lambda i: (i, 0, 0, 0)),
        out_shape=jax.ShapeDtypeStruct((t, 2, PEER_SUB, 128), F32),
        compiler_params=_cparams("parallel"),
        name="peer_out",
    )(eid, act, res_split, tab_packed)


def peer_layer(x, ng, w_q, keys, u_packed, v_packed):
    t, d = x.shape
    scores, xn = peer_scores(x, ng, w_q.astype(jnp.bfloat16), keys.astype(jnp.bfloat16))
    eid_f, gate = peer_topk(scores)
    eid = eid_f.astype(jnp.int32)
    hidden = peer_hidden(eid, xn.reshape(t, 2, PEER_SUB, 128), u_packed).reshape(t, PEER_SEL)
    act = jax.nn.gelu(hidden, approximate=False) * gate
    out = peer_out(eid, act, x.reshape(t, 2, PEER_SUB, 128), v_packed)
    return out.reshape(t, d)


def mixer_layer(l, x, states, mp):
    xn = rmsnorm(x, mp[0])
    if l % 2 == 0:
        out, *ns = hybrid_mixer(xn, *states, *mp[1:])
    else:
        out, *ns = rwkv7_mixer(xn, *states, *mp[1:])
    return x + out, ns


def run_trunks(xs, states_list, layer_params, peer_params, final_norm):
    new_states = [[] for _ in xs]
    sizes = [x.shape[0] * x.shape[1] for x in xs]
    offs = np.cumsum([0] + sizes)
    for l in range(DEPTH):
        mixed = []
        for gi, x in enumerate(xs):
            x, ns = mixer_layer(l, x, states_list[gi][l], layer_params[l])
            new_states[gi].extend(ns)
            mixed.append(x.reshape(-1, D_MODEL))
        ng, wq, keys, u_packed, v_packed = peer_params[l]
        y = peer_layer(jnp.concatenate(mixed, axis=0), ng, wq, keys, u_packed, v_packed)
        xs = [y[offs[gi]:offs[gi + 1]].reshape(xs[gi].shape) for gi in range(len(xs))]
    ys = [pallas_rmsnorm(x, final_norm) for x in xs]
    return ys, new_states


def kernel(x_prompt, x_sample, cache_sb_k, cache_sb_v, state_ssd_conv, state_ssd,
           state_rwkv_shift, state_rwkv_wkv,
           l0_norm_mix, l0_w_in, l0_conv_w, l0_conv_b, l0_dt_bias, l0_a_log, l0_d_skip,
           l0_ssd_norm, l0_w_out,
           l1_norm_mix, l1_mu, l1_w_rkv, l1_w0, l1_w1, l1_w2, l1_a0, l1_a1, l1_a2,
           l1_g1, l1_g2, l1_k_k, l1_k_a, l1_r_k, l1_lnx_w, l1_lnx_b, l1_w_out,
           l0_norm_ffn, l0_peer_wq, l0_peer_keys, l0_peer_u, l0_peer_v,
           l1_norm_ffn, l1_peer_wq, l1_peer_keys, l1_peer_u, l1_peer_v,
           final_norm):
    layer_params = [
        (l0_norm_mix, l0_w_in, l0_conv_w, l0_conv_b, l0_dt_bias, l0_a_log, l0_d_skip,
         l0_ssd_norm, l0_w_out),
        (l1_norm_mix, l1_mu, l1_w_rkv, l1_w0, l1_w1, l1_w2, l1_a0, l1_a1, l1_a2,
         l1_g1, l1_g2, l1_k_k, l1_k_a, l1_r_k, l1_lnx_w, l1_lnx_b, l1_w_out),
    ]
    peer_params = [
        (l0_norm_ffn, l0_peer_wq, l0_peer_keys, pack_table(l0_peer_u), pack_table(l0_peer_v)),
        (l1_norm_ffn, l1_peer_wq, l1_peer_keys, pack_table(l1_peer_u), pack_table(l1_peer_v)),
    ]
    bp = x_prompt.shape[0]
    dt = x_prompt.dtype
    zero_states = [
        (jnp.zeros((bp, 0, SB_HEADS, SB_HEAD_DIM), dt), jnp.zeros((bp, 0, SB_HEADS, SB_HEAD_DIM), dt),
         jnp.zeros((bp, SSD_CONV - 1, SSD_CONV_DIM), dt),
         jnp.zeros((bp, SSD_HEADS, SSD_HEAD_DIM, SSD_STATE), dt)),
        (jnp.zeros((bp, 1, D_MODEL), dt), jnp.zeros((bp, RW_HEADS, RW_HEAD, RW_HEAD), dt)),
    ]
    sample_states = [
        (cache_sb_k, cache_sb_v, state_ssd_conv, state_ssd),
        (state_rwkv_shift, state_rwkv_wkv),
    ]
    (y_prompt, y_sample), (p_states, s_states) = run_trunks(
        [x_prompt, x_sample], [zero_states, sample_states], layer_params, peer_params, final_norm)
    p_sb_k, p_sb_v, p_conv, p_ssd, p_shift, p_wkv = p_states
    s_sb_k, s_sb_v, s_conv, s_ssd, s_shift, s_wkv = s_states
    return (y_prompt, y_sample, p_sb_k, p_sb_v, p_conv, p_ssd, p_shift, p_wkv,
            s_sb_k, s_sb_v, s_conv, s_ssd, s_shift, s_wkv)
```

```python
import functools
import math
import jax, jax.numpy as jnp
from jax import lax
import numpy as np
from jax.experimental import pallas as pl
from jax.experimental.pallas import tpu as pltpu

D_MODEL = 1024
DEPTH = 2

CHUNK = 64
EPS = 1e-6
SB_HEADS = 8
SB_HEAD_DIM = 64
SB_WIDTH = SB_HEADS * SB_HEAD_DIM
SB_BLOCK = 128
SSD_HEADS = 8
SSD_HEAD_DIM = 64
SSD_INNER = SSD_HEADS * SSD_HEAD_DIM
SSD_GROUPS = 2
SSD_STATE = 64
SSD_CONV = 4
SSD_CONV_DIM = SSD_INNER + 2 * SSD_GROUPS * SSD_STATE
SSD_CHUNK = CHUNK
HYB_SPLITS = (SB_WIDTH, 2 * SB_WIDTH, 3 * SB_WIDTH, 3 * SB_WIDTH + SSD_INNER,
              3 * SB_WIDTH + SSD_INNER + SSD_CONV_DIM)
HYB_IN = 3 * SB_WIDTH + SSD_INNER + SSD_CONV_DIM + SSD_HEADS
RW_HEAD = 64
RW_HEADS = D_MODEL // RW_HEAD
RW_LNX_EPS = 64e-5
PEER_HEADS = 8
PEER_NKEYS = 128
PEER_QDIM = 256
PEER_TOPK = 16
PEER_BLOCK = 128

F32 = jnp.float32
V7X_VMEM_BYTES = 64 * 1024 * 1024


def _cparams(*semantics):
    return pltpu.CompilerParams(dimension_semantics=semantics, vmem_limit_bytes=V7X_VMEM_BYTES)


def rmsnorm(x, g):
    xf = x.astype(F32)
    y = xf * lax.rsqrt(jnp.mean(xf * xf, axis=-1, keepdims=True) + EPS)
    return (y * g.astype(F32)).astype(x.dtype)


def _rmsnorm_kernel(x_ref, g_ref, o_ref):
    x = x_ref[...]
    y = x * lax.rsqrt(jnp.mean(x * x, axis=-1, keepdims=True) + EPS)
    o_ref[...] = y * g_ref[...]


def pallas_rmsnorm(x, g, rows=256):
    shp = x.shape
    x2 = x.reshape(-1, shp[-1])
    t, d = x2.shape
    rows = min(rows, t)
    out = pl.pallas_call(
        _rmsnorm_kernel,
        grid=(t // rows,),
        in_specs=[pl.BlockSpec((rows, d), lambda i: (i, 0)),
                  pl.BlockSpec((1, d), lambda i: (0, 0))],
        out_specs=pl.BlockSpec((rows, d), lambda i: (i, 0)),
        out_shape=jax.ShapeDtypeStruct((t, d), x.dtype),
        compiler_params=_cparams("parallel"),
        name="final_rmsnorm",
    )(x2, g.reshape(1, d))
    return out.reshape(shp)


SB_KBLOCK = 128
SB_LOG_FLOOR = -90.0


def _split3(x):
    h = x.astype(jnp.bfloat16)
    r = x - h.astype(F32)
    m = r.astype(jnp.bfloat16)
    l = (r - m.astype(F32)).astype(jnp.bfloat16)
    return h, m, l


def _sb_attn_kernel(q_ref, k_ref, v_ref, o_ref, *, tq, past):
    iq = pl.program_id(2)
    q = q_ref[0, 0]
    q_pos = past + iq * tq + lax.broadcasted_iota(jnp.int32, (tq, SB_KBLOCK), 0)
    k_off = lax.broadcasted_iota(jnp.int32, (tq, SB_KBLOCK), 1)
    r = lax.broadcasted_iota(jnp.int32, (SB_KBLOCK, SB_KBLOCK), 0)
    c = lax.broadcasted_iota(jnp.int32, (SB_KBLOCK, SB_KBLOCK), 1)
    upper = (r > c).astype(jnp.bfloat16)
    j0 = (past + (iq + 1) * tq - 1) // SB_KBLOCK

    def cond(state):
        j, live, _, _ = state
        return jnp.logical_and(j >= 0, live > 0)

    def body(state):
        j, _, carry, acc = state
        start = pl.multiple_of(j * SB_KBLOCK, SB_KBLOCK)
        kb = k_ref[0, 0, pl.ds(start, SB_KBLOCK), :]
        vb = v_ref[0, 0, pl.ds(start, SB_KBLOCK), :]
        z = lax.dot_general(q, kb, (((1,), (1,)), ((), ())), preferred_element_type=F32)
        mask = (j * SB_KBLOCK + k_off) < q_pos
        sp = jnp.maximum(z, 0.0) + jnp.log1p(jnp.exp(-jnp.abs(z)))
        log_fail = jnp.where(mask, -sp, 0.0)
        h, m, l = _split3(log_fail)
        later = (jnp.dot(h, upper, preferred_element_type=F32)
                 + jnp.dot(m, upper, preferred_element_type=F32)
                 + jnp.dot(l, upper, preferred_element_type=F32))
        log_w = (z - sp) + later + carry
        w = jnp.where(mask, jnp.exp(log_w), 0.0)
        acc = acc + jnp.dot(w.astype(jnp.bfloat16), vb, preferred_element_type=F32)
        carry = carry + jnp.sum(log_fail, axis=-1, keepdims=True)
        live = (jnp.max(carry) > SB_LOG_FLOOR).astype(jnp.int32)
        return j - 1, live, carry, acc

    state = (j0, jnp.int32(1), jnp.zeros((tq, 1), F32), jnp.zeros((tq, SB_HEAD_DIM), F32))
    _, _, _, acc = lax.while_loop(cond, body, state)
    o_ref[0, 0] = acc


def sb_attention(q, k_all, v_all, past):
    b, lq, h, d = q.shape
    s = k_all.shape[1]
    tq = min(lq, 128)
    s_pad = -(-s // SB_KBLOCK) * SB_KBLOCK
    qh = (q * (d ** -0.5)).astype(jnp.bfloat16).transpose(0, 2, 1, 3)
    kh = jnp.pad(k_all.astype(jnp.bfloat16).transpose(0, 2, 1, 3), ((0, 0), (0, 0), (0, s_pad - s), (0, 0)))
    vh = jnp.pad(v_all.astype(jnp.bfloat16).transpose(0, 2, 1, 3), ((0, 0), (0, 0), (0, s_pad - s), (0, 0)))
    out = pl.pallas_call(
        functools.partial(_sb_attn_kernel, tq=tq, past=past),
        grid=(b, h, lq // tq),
        in_specs=[pl.BlockSpec((1, 1, tq, d), lambda bi, hi, qi: (bi, hi, qi, 0)),
                  pl.BlockSpec((1, 1, s_pad, d), lambda bi, hi, qi: (bi, hi, 0, 0)),
                  pl.BlockSpec((1, 1, s_pad, d), lambda bi, hi, qi: (bi, hi, 0, 0))],
        out_specs=pl.BlockSpec((1, 1, tq, d), lambda bi, hi, qi: (bi, hi, qi, 0)),
        out_shape=jax.ShapeDtypeStruct((b, h, lq, d), F32),
        compiler_params=_cparams("parallel", "parallel", "arbitrary"),
        name="sb_attention",
    )(qh, kh, vh)
    return out.transpose(0, 2, 1, 3)


def causal_conv(u, buf, w, bias):
    L = u.shape[1]
    full = jnp.concatenate([buf, u], axis=1)
    out = bias
    for i in range(SSD_CONV):
        out = out + full[:, i:i + L] * w[i]
    return out, full[:, -(SSD_CONV - 1):]


def ssd_scan(x, dt, a, bm, cm, h0):
    b, L, H, P = x.shape
    Q = SSD_CHUNK if L % SSD_CHUNK == 0 else L
    nc = L // Q
    rep = H // SSD_GROUPS
    xf = x.astype(F32).reshape(b, nc, Q, H, P)
    bh = jnp.repeat(bm.astype(F32), rep, axis=2).reshape(b, nc, Q, H, SSD_STATE)
    ch = jnp.repeat(cm.astype(F32), rep, axis=2).reshape(b, nc, Q, H, SSD_STATE)
    dtc = dt.reshape(b, nc, Q, H)
    a_cum = jnp.cumsum(dtc * a, axis=2)
    seg = a_cum[:, :, :, None, :] - a_cum[:, :, None, :, :]
    causal = jnp.tril(jnp.ones((Q, Q), dtype=bool))[None, None, :, :, None]
    decay = jnp.exp(jnp.where(causal, seg, -jnp.inf))
    cb = jnp.einsum('bclhn,bcshn->bclsh', ch, bh)
    y_diag = jnp.einsum('bclsh,bcshp->bclhp', cb * decay * dtc[:, :, None], xf)
    w_state = jnp.exp(a_cum[:, :, -1:] - a_cum) * dtc
    chunk_states = jnp.einsum('bclhn,bclh,bclhp->bchpn', bh, w_state, xf)
    chunk_decay = jnp.exp(a_cum[:, :, -1])

    def step(h, inp):
        dec, st = inp
        return dec[:, :, None, None] * h + st, h

    h_last, h_prev = lax.scan(step, h0, (chunk_decay.transpose(1, 0, 2),
                                          chunk_states.transpose(1, 0, 2, 3, 4)))
    h_prev = h_prev.transpose(1, 0, 2, 3, 4)
    y_off = jnp.einsum('bclhn,bchpn->bclhp', ch, h_prev) * jnp.exp(a_cum)[..., None]
    return (y_diag + y_off).reshape(b, L, H, P), h_last


def hybrid_mixer(xn, k_cache, v_cache, conv_buf, ssd_h,
                 w_in, conv_w, conv_b, dt_bias, a_log, d_skip, ssd_norm, w_out):
    b, L, _ = xn.shape
    proj = xn @ w_in
    q, k, v, z, xbc, dt = jnp.split(proj, HYB_SPLITS, axis=-1)
    q = q.reshape(b, L, SB_HEADS, SB_HEAD_DIM)
    k = k.reshape(b, L, SB_HEADS, SB_HEAD_DIM)
    v = v.reshape(b, L, SB_HEADS, SB_HEAD_DIM)
    past = k_cache.shape[1]
    k_all = jnp.concatenate([k_cache, k], axis=1)
    v_all = jnp.concatenate([v_cache, v], axis=1)
    o_sb = sb_attention(q, k_all, v_all, past).reshape(b, L, SB_WIDTH)
    xbc_c, new_buf = causal_conv(xbc, conv_buf, conv_w, conv_b)
    xbc_c = jax.nn.silu(xbc_c)
    xs, bm, cm = jnp.split(xbc_c, (SSD_INNER, SSD_INNER + SSD_GROUPS * SSD_STATE), axis=-1)
    xs = xs.reshape(b, L, SSD_HEADS, SSD_HEAD_DIM)
    bm = bm.reshape(b, L, SSD_GROUPS, SSD_STATE)
    cm = cm.reshape(b, L, SSD_GROUPS, SSD_STATE)
    dtp = jax.nn.softplus(dt.astype(F32) + dt_bias.astype(F32))
    a = -jnp.exp(a_log.astype(F32))
    y, h_new = ssd_scan(xs, dtp, a, bm, cm, ssd_h.astype(F32))
    y = y + d_skip.astype(F32)[:, None] * xs.astype(F32)
    y = y.reshape(b, L, SSD_INNER) * jax.nn.silu(z.astype(F32))
    yg = y.reshape(b, L, SSD_GROUPS, SSD_INNER // SSD_GROUPS)
    yg = yg * lax.rsqrt(jnp.mean(yg * yg, axis=-1, keepdims=True) + EPS)
    y = yg.reshape(b, L, SSD_INNER) * ssd_norm.astype(F32)
    out = jnp.concatenate([o_sb, y.astype(xn.dtype)], axis=-1) @ w_out
    return out, k, v, new_buf, h_new.astype(ssd_h.dtype)


RW_PAIRS = RW_HEADS // 2
RW_NP = 4
RW_TC = 64


def _rwkv_scan_kernel(r_ref, d_ref, k_ref, v_ref, a_ref, b_ref, s0_ref, o_ref, st_ref, s_scr, vcol_scr, *, tc):
    c = pl.program_id(2)

    @pl.when(c == 0)
    def _():
        s_scr[...] = s0_ref[0]

    row = lax.broadcasted_iota(jnp.int32, (RW_HEAD, 128), 0)
    lane = lax.broadcasted_iota(jnp.int32, (RW_HEAD, 128), 1)
    left = lane < RW_HEAD
    diag = (lane & (RW_HEAD - 1)) == row
    same_head = ((lax.broadcasted_iota(jnp.int32, (128, 128), 0) < RW_HEAD)
                 == (lax.broadcasted_iota(jnp.int32, (128, 128), 1) < RW_HEAD)).astype(jnp.bfloat16)

    def half_sums(x):
        lo = jnp.sum(jnp.where(left, x, 0.0), axis=1, keepdims=True)
        hi = jnp.sum(jnp.where(left, 0.0, x), axis=1, keepdims=True)
        return jnp.where(left, lo, hi)

    def vcol_body(t8, carry):
        base = pl.multiple_of(t8 * 8, 8)
        for p in range(RW_NP):
            terms = _terms3(v_ref[0, pl.ds(base, 8), p * 128:(p + 1) * 128])
            for i in range(8):
                h, m, l = (jnp.where(diag, t[i:i + 1], 0.0).astype(jnp.bfloat16) for t in terms)
                vcol_scr[p, base + i] = (jnp.dot(h, same_head, preferred_element_type=F32)
                                         + jnp.dot(m, same_head, preferred_element_type=F32)
                                         + jnp.dot(l, same_head, preferred_element_type=F32))
        return carry

    lax.fori_loop(0, tc // 8, vcol_body, 0)

    def step8(t8, carry):
        base = pl.multiple_of(t8 * 8, 8)
        for p in range(RW_NP):
            sl = slice(p * 128, (p + 1) * 128)
            a8 = a_ref[0, pl.ds(base, 8), sl]
            d8 = d_ref[0, pl.ds(base, 8), sl]
            b8 = b_ref[0, pl.ds(base, 8), sl]
            k8 = k_ref[0, pl.ds(base, 8), sl]
            r8 = r_ref[0, pl.ds(base, 8), sl]
            s = s_scr[p]
            rows = []
            for i in range(8):
                sa = half_sums(s * a8[i:i + 1])
                s = s * d8[i:i + 1] + sa * b8[i:i + 1] + vcol_scr[p, base + i] * k8[i:i + 1]
                oc = half_sums(s * r8[i:i + 1])
                rows.append(jnp.sum(jnp.where(diag, oc, 0.0), axis=0, keepdims=True))
            s_scr[p] = s
            o_ref[0, pl.ds(base, 8), sl] = jnp.concatenate(rows, axis=0)
        return carry

    lax.fori_loop(0, tc // 8, step8, 0)

    @pl.when(c == pl.num_programs(2) - 1)
    def _():
        st_ref[0] = s_scr[...]


def rwkv_scan(r, d, k, v, a, b, wkv):
    bsz, L, D = r.shape
    tc = min(RW_TC, L)
    s0 = wkv.reshape(bsz, RW_PAIRS, 2, RW_HEAD, RW_HEAD).transpose(0, 1, 3, 2, 4).reshape(bsz, RW_PAIRS, RW_HEAD, 128)
    seq = pl.BlockSpec((1, tc, RW_NP * 128), lambda bi, g, c: (bi, c, g))
    st = pl.BlockSpec((1, RW_NP, RW_HEAD, 128), lambda bi, g, c: (bi, g, 0, 0))
    o, s_new = pl.pallas_call(
        functools.partial(_rwkv_scan_kernel, tc=tc),
        grid=(bsz, RW_PAIRS // RW_NP, L // tc),
        in_specs=[seq] * 6 + [st],
        out_specs=[seq, st],
        out_shape=[jax.ShapeDtypeStruct((bsz, L, D), F32),
                   jax.ShapeDtypeStruct((bsz, RW_PAIRS, RW_HEAD, 128), F32)],
        scratch_shapes=[pltpu.VMEM((RW_NP, RW_HEAD, 128), F32),
                        pltpu.VMEM((RW_NP, tc, RW_HEAD, 128), F32)],
        compiler_params=_cparams("parallel", "parallel", "arbitrary"),
        name="rwkv_scan",
    )(r, d, k, v, a, b, s0)
    s_new = s_new.reshape(bsz, RW_PAIRS, RW_HEAD, 2, RW_HEAD).transpose(0, 1, 3, 2, 4).reshape(bsz, RW_HEADS, RW_HEAD, RW_HEAD)
    return o, s_new


def _proj3_kernel(x_ref, w_ref, o_ref):
    o_ref[0] = jnp.dot(x_ref[0], w_ref[0], preferred_element_type=F32)


def proj3(x3, w3):
    _, m, kdim = x3.shape
    n = w3.shape[2]
    tm = min(m, 512)
    tn = min(n, 512)
    return pl.pallas_call(
        _proj3_kernel,
        grid=(3, m // tm, n // tn),
        in_specs=[pl.BlockSpec((1, tm, kdim), lambda i, a, b: (i, a, 0)),
                  pl.BlockSpec((1, kdim, tn), lambda i, a, b: (i, 0, b))],
        out_specs=pl.BlockSpec((1, tm, tn), lambda i, a, b: (i, a, b)),
        out_shape=jax.ShapeDtypeStruct((3, m, n), F32),
        compiler_params=_cparams("parallel", "parallel", "parallel"),
        name="rwkv_proj3",
    )(x3, w3)


def rwkv7_mixer(xn, shift_buf, wkv, mu, w_rkv, w0, w1, w2, a0, a1, a2, g1, g2,
                k_k, k_a, r_k, lnx_w, lnx_b, w_out):
    b, L, D = xn.shape
    x_prev = jnp.concatenate([shift_buf, xn[:, :-1]], axis=1)
    xx = x_prev - xn
    xm = xn[None] + xx[None] * mu[:, None, None, :]
    r, k, v = proj3(xm[:3].astype(jnp.bfloat16).reshape(3, b * L, D),
                    w_rkv.astype(jnp.bfloat16)).reshape(3, b, L, D)
    xw, xa, xg = xm[3], xm[4], xm[5]
    w = -jax.nn.softplus(-(w0 + jnp.tanh(xw @ w1) @ w2).astype(F32)) - 0.5
    decay = jnp.exp(-jnp.exp(w))
    a = jax.nn.sigmoid((a0 + (xa @ a1) @ a2).astype(F32))
    g = (jax.nn.sigmoid(xg @ g1) @ g2).astype(F32)
    hs = (b, L, RW_HEADS, RW_HEAD)
    kk = (k * k_k.astype(F32)).reshape(hs)
    kk = kk / jnp.maximum(jnp.sqrt(jnp.sum(kk * kk, axis=-1, keepdims=True)), 1e-12)
    k = k * (1.0 + (a - 1.0) * k_a.astype(F32))
    rh, dh, kh, vh, ah = (t.reshape(hs) for t in (r, decay, k, v, a))
    a_vec = -kk
    b_vec = kk * ah
    o, S_last = rwkv_scan(r, decay, k, v, a_vec.reshape(b, L, D), b_vec.reshape(b, L, D), wkv.astype(F32))
    o = o.reshape(hs)
    mean = jnp.mean(o, axis=-1, keepdims=True)
    var = jnp.mean(jnp.square(o - mean), axis=-1, keepdims=True)
    o = ((o - mean) * lax.rsqrt(var + RW_LNX_EPS)).reshape(b, L, D)
    o = o * lnx_w.astype(F32) + lnx_b.astype(F32)
    bonus = jnp.sum(rh * kh * r_k.astype(F32), axis=-1, keepdims=True) * vh
    o = o + bonus.reshape(b, L, D)
    out = (o * g).astype(xn.dtype) @ w_out
    return out, xn[:, -1:], S_last.astype(wkv.dtype)


PEER_TM = 256
PEER_TB = 32
PEER_NSTAGE = 4
PEER_HALF = D_MODEL // 2
PEER_SUB = PEER_HALF // 128
PEER_SEL = PEER_HEADS * PEER_TOPK


def _peer_scores_kernel(x_ref, g_ref, wq_ref, keys_ref, s_ref, xn_ref, xnb_ref):
    @pl.when(pl.program_id(1) == 0)
    def _():
        x = x_ref[...]
        xn = x * lax.rsqrt(jnp.mean(x * x, axis=-1, keepdims=True) + EPS) * g_ref[...]
        xn_ref[...] = xn
        xnb_ref[...] = xn.astype(jnp.bfloat16)

    q = jnp.dot(xnb_ref[...], wq_ref[...], preferred_element_type=F32).astype(jnp.bfloat16)
    half = PEER_QDIM // 2
    for i in range(2):
        s_ref[i * PEER_NKEYS:(i + 1) * PEER_NKEYS, :] = lax.dot_general(
            keys_ref[0, i], q[:, i * half:(i + 1) * half],
            (((1,), (1,)), ((), ())), preferred_element_type=F32)


def peer_scores(x, g, wq_bf, keys_bf):
    t, d = x.shape
    return pl.pallas_call(
        _peer_scores_kernel,
        grid=(t // PEER_TM, PEER_HEADS),
        in_specs=[pl.BlockSpec((PEER_TM, d), lambda i, h: (i, 0)),
                  pl.BlockSpec((1, d), lambda i, h: (0, 0)),
                  pl.BlockSpec((d, PEER_QDIM), lambda i, h: (0, h)),
                  pl.BlockSpec((1, 2, PEER_NKEYS, PEER_QDIM // 2), lambda i, h: (h, 0, 0, 0))],
        out_specs=[pl.BlockSpec((2 * PEER_NKEYS, PEER_TM), lambda i, h: (h, i)),
                   pl.BlockSpec((PEER_TM, d), lambda i, h: (i, 0))],
        out_shape=[jax.ShapeDtypeStruct((PEER_HEADS * 2 * PEER_NKEYS, t), F32),
                   jax.ShapeDtypeStruct((t, d), F32)],
        scratch_shapes=[pltpu.VMEM((PEER_TM, d), jnp.bfloat16)],
        compiler_params=_cparams("parallel", "arbitrary"),
        name="peer_scores",
    )(x, g.reshape(1, d), wq_bf, keys_bf)


PEER_TT = 128
NEG_INF = float("-inf")


def _peer_candidates():
    k = PEER_TOPK
    a_idx, b_idx, valid = [], [], []
    for a in range(k // 2):
        nb = k // (a + 1)
        rows = k if a == 0 else 8
        for b in range(rows):
            a_idx.append(a); b_idx.append(b); valid.append(b < nb)
    for a in range(k // 2, k):
        a_idx.append(a); b_idx.append(0); valid.append(True)
    return np.array(a_idx), np.array(b_idx), np.array(valid)


_CAND_A, _CAND_B, _CAND_VALID = _peer_candidates()
PEER_NCAND = len(_CAND_A)


def _top_rows(x, tag, k):
    big = jnp.float32(1e9)
    vals, tags = [], []
    for _ in range(k):
        m = jnp.max(x, axis=0, keepdims=True)
        t = jnp.min(jnp.where(x == m, tag, big), axis=0, keepdims=True)
        vals.append(m)
        tags.append(t)
        x = jnp.where(tag == t, NEG_INF, x)
    return vals, tags


def _peer_topk_kernel(s_ref, pos_ref, eid_ref, gate_ref, v_scr, i_scr, e_scr, g_scr):
    k = PEER_TOPK
    key_tag = lax.broadcasted_iota(jnp.int32, (PEER_NKEYS, PEER_TT), 0).astype(F32)

    def half_body(g, carry):
        start = pl.multiple_of(g * PEER_NKEYS, PEER_NKEYS)
        vals, tags = _top_rows(s_ref[pl.ds(start, PEER_NKEYS), :], key_tag, k)
        v_scr[g] = jnp.concatenate(vals, axis=0)
        i_scr[g] = jnp.concatenate(tags, axis=0)
        return carry

    lax.fori_loop(0, 2 * PEER_HEADS, half_body, 0)

    pos = pos_ref[...]
    valid = pos >= 0.0

    def head_body(h, carry):
        s1 = v_scr[2 * h]; s2 = v_scr[2 * h + 1]
        i1 = i_scr[2 * h]; i2 = i_scr[2 * h + 1]
        e2 = i2[0:8]
        pieces_s = [s1[0:1] + s2]
        pieces_e = [i1[0:1] * PEER_NKEYS + i2]
        for a in range(1, k // 2):
            pieces_s.append(s1[a:a + 1] + s2[0:8])
            pieces_e.append(i1[a:a + 1] * PEER_NKEYS + e2)
        pieces_s.append(s1[k // 2:k] + s2[0:1])
        pieces_e.append(i1[k // 2:k] * PEER_NKEYS + i2[0:1])
        cand = jnp.where(valid, jnp.concatenate(pieces_s, axis=0), NEG_INF)
        cid = jnp.concatenate(pieces_e, axis=0)
        tops, tags = _top_rows(cand, pos, k)
        eids = [jnp.max(jnp.where(pos == t, cid, -1.0), axis=0, keepdims=True) for t in tags]
        top = jnp.concatenate(tops, axis=0)
        p = jnp.exp(top - top[0:1])
        gate = p / jnp.sum(p, axis=0, keepdims=True)
        row = pl.multiple_of(h * k, k)
        e_scr[pl.ds(row, k), :] = jnp.concatenate(eids, axis=0)
        g_scr[pl.ds(row, k), :] = gate
        return carry

    lax.fori_loop(0, PEER_HEADS, head_body, 0)
    eid_ref[...] = e_scr[...].T
    gate_ref[...] = g_scr[...].T


def peer_topk(scores_t):
    t = scores_t.shape[1]
    pos = np.where(_CAND_VALID, _CAND_A * PEER_TOPK + _CAND_B, -1).astype(np.float32)
    pos = jnp.asarray(np.broadcast_to(pos[:, None], (PEER_NCAND, PEER_TT)))
    return pl.pallas_call(
        _peer_topk_kernel,
        grid=(t // PEER_TT,),
        in_specs=[pl.BlockSpec((2 * PEER_HEADS * PEER_NKEYS, PEER_TT), lambda i: (0, i)),
                  pl.BlockSpec((PEER_NCAND, PEER_TT), lambda i: (0, 0))],
        out_specs=[pl.BlockSpec((PEER_TT, PEER_SEL), lambda i: (i, 0)),
                   pl.BlockSpec((PEER_TT, PEER_SEL), lambda i: (i, 0))],
        out_shape=[jax.ShapeDtypeStruct((t, PEER_SEL), F32),
                   jax.ShapeDtypeStruct((t, PEER_SEL), F32)],
        scratch_shapes=[pltpu.VMEM((2 * PEER_HEADS, PEER_TOPK, PEER_TT), F32),
                        pltpu.VMEM((2 * PEER_HEADS, PEER_TOPK, PEER_TT), F32),
                        pltpu.VMEM((PEER_SEL, PEER_TT), F32),
                        pltpu.VMEM((PEER_SEL, PEER_TT), F32)],
        compiler_params=_cparams("parallel"),
        name="peer_topk",
    )(scores_t, pos)


def pack_table(tab):
    e = tab.shape[0]
    b = lax.bitcast_convert_type(tab.astype(jnp.bfloat16), jnp.uint16).astype(jnp.uint32)
    b = b.reshape(e, 2, PEER_SUB, 128)
    return (b[:, 0] << 16) | b[:, 1]


def _terms3(x):
    h = x.astype(jnp.bfloat16).astype(F32)
    r = x - h
    m = r.astype(jnp.bfloat16).astype(F32)
    l = (r - m).astype(jnp.bfloat16).astype(F32)
    return h, m, l


def _stack_terms(terms, s):
    row = lax.broadcasted_iota(jnp.int32, (8, 128), 0)
    h, m, l = (t[s:s + 1] for t in terms)
    return jnp.where(row == 0, h, jnp.where(row == 1, m, jnp.where(row == 2, l, 0.0))).astype(jnp.bfloat16)


def _stage_rows(eid_ref, tab_ref, w_scr, t):
    for j in range(PEER_SEL):
        w_scr[PEER_SUB * j:PEER_SUB * (j + 1), :] = tab_ref[eid_ref[t, j]]


def _token_groups(eid_ref, tab_ref, w_scrs, compute):
    n = len(w_scrs)

    def group(i, carry):
        for k in range(n):
            _stage_rows(eid_ref, tab_ref, w_scrs[k], n * i + k)
        for k in range(n):
            compute(n * i + k, w_scrs[k])
        return carry

    lax.fori_loop(0, PEER_TB // n, group, 0)


def _halves(w_scr, s):
    w = w_scr[pl.ds(s, PEER_SEL, stride=PEER_SUB), :]
    hi = lax.bitcast_convert_type(w & jnp.uint32(0xFFFF0000), F32).astype(jnp.bfloat16)
    lo = lax.bitcast_convert_type(w << 16, F32).astype(jnp.bfloat16)
    return hi, lo


def _peer_hidden_kernel(eid_ref, x_ref, tab_ref, hid_ref, *w_scrs):
    nt = (((1,), (1,)), ((), ()))

    def compute(t, w_scr):
        th = _terms3(x_ref[t, 0])
        tl = _terms3(x_ref[t, 1])
        acc = jnp.zeros((8, PEER_SEL), F32)
        for s in range(PEER_SUB):
            hi, lo = _halves(w_scr, s)
            acc = acc + lax.dot_general(_stack_terms(th, s), hi, nt, preferred_element_type=F32)
            acc = acc + lax.dot_general(_stack_terms(tl, s), lo, nt, preferred_element_type=F32)
        hid_ref[t] = acc[0:1] + acc[1:2] + acc[2:3]

    _token_groups(eid_ref, tab_ref, w_scrs, compute)


def _peer_out_kernel(eid_ref, act_ref, res_ref, tab_ref, out_ref, *w_scrs):
    def compute(t, w_scr):
        a8 = _stack_terms(_terms3(act_ref[t]), 0)
        for s in range(PEER_SUB):
            hi, lo = _halves(w_scr, s)
            yh = jnp.dot(a8, hi, preferred_element_type=F32)
            yl = jnp.dot(a8, lo, preferred_element_type=F32)
            out_ref[t, 0, s:s + 1, :] = res_ref[t, 0, s:s + 1, :] + (yh[0:1] + yh[1:2] + yh[2:3])
            out_ref[t, 1, s:s + 1, :] = res_ref[t, 1, s:s + 1, :] + (yl[0:1] + yl[1:2] + yl[2:3])

    _token_groups(eid_ref, tab_ref, w_scrs, compute)


def _table_spec(n_exp):
    return pl.BlockSpec((n_exp, PEER_SUB, 128), lambda i: (0, 0, 0), pipeline_mode=pl.Buffered(1))


def peer_hidden(eid, xn_split, tab_packed):
    t = eid.shape[0]
    n_exp = tab_packed.shape[0]
    return pl.pallas_call(
        _peer_hidden_kernel,
        grid=(t // PEER_TB,),
        in_specs=[pl.BlockSpec((PEER_TB, PEER_SEL), lambda i: (i, 0), memory_space=pltpu.SMEM),
                  pl.BlockSpec((PEER_TB, 2, PEER_SUB, 128), lambda i: (i, 0, 0, 0)),
                  _table_spec(n_exp)],
        out_specs=pl.BlockSpec((PEER_TB, 1, PEER_SEL), lambda i: (i, 0, 0)),
        out_shape=jax.ShapeDtypeStruct((t, 1, PEER_SEL), F32),
        scratch_shapes=[pltpu.VMEM((PEER_SEL * PEER_SUB, 128), jnp.uint32)] * PEER_NSTAGE,
        compiler_params=_cparams("parallel"),
        name="peer_hidden",
    )(eid, xn_split, tab_packed)


def peer_out(eid, act, res_split, tab_packed):
    t = eid.shape[0]
    n_exp = tab_packed.shape[0]
    return pl.pallas_call(
        _peer_out_kernel,
        grid=(t // PEER_TB,),
        in_specs=[pl.BlockSpec((PEER_TB, PEER_SEL), lambda i: (i, 0), memory_space=pltpu.SMEM),
                  pl.BlockSpec((PEER_TB, 1, PEER_SEL), lambda i: (i, 0, 0)),
                  pl.BlockSpec((PEER_TB, 2, PEER_SUB, 128), lambda i: (i, 0, 0, 0)),
                  _table_spec(n_exp)],
        out_specs=pl.BlockSpec((PEER_TB, 2, PEER_SUB, 128), lambda i: (i, 0, 0, 0)),
        out_shape=jax.ShapeDtypeStruct((t, 2, PEER_SUB, 128), F32),
        scratch_shapes=[pltpu.VMEM((PEER_SEL * PEER_SUB, 128), jnp.uint32)] * PEER_NSTAGE,
        compiler_params=_cparams("parallel"),
        name="peer_out",
    )(eid, act.reshape(t, 1, PEER_SEL), res_split, tab_packed)


def peer_layer(x, ng, w_q, keys, u_packed, v_packed):
    t, d = x.shape
    scores, xn = peer_scores(x, ng, w_q.astype(jnp.bfloat16), keys.astype(jnp.bfloat16))
    eid_f, gate = peer_topk(scores)
    eid = eid_f.astype(jnp.int32)
    hidden = peer_hidden(eid, xn.reshape(t, 2, PEER_SUB, 128), u_packed).reshape(t, PEER_SEL)
    act = jax.nn.gelu(hidden, approximate=False) * gate
    out = peer_out(eid, act, x.reshape(t, 2, PEER_SUB, 128), v_packed)
    return out.reshape(t, d)


def mixer_layer(l, x, states, mp):
    xn = rmsnorm(x, mp[0])
    if l % 2 == 0:
        out, *ns = hybrid_mixer(xn, *states, *mp[1:])
    else:
        out, *ns = rwkv7_mixer(xn, *states, *mp[1:])
    return x + out, ns


def run_trunks(xs, states_list, layer_params, peer_params, final_norm):
    new_states = [[] for _ in xs]
    sizes = [x.shape[0] * x.shape[1] for x in xs]
    offs = np.cumsum([0] + sizes)
    for l in range(DEPTH):
        mixed = []
        for gi, x in enumerate(xs):
            x, ns = mixer_layer(l, x, states_list[gi][l], layer_params[l])
            new_states[gi].extend(ns)
            mixed.append(x.reshape(-1, D_MODEL))
        ng, wq, keys, u_packed, v_packed = peer_params[l]
        y = peer_layer(jnp.concatenate(mixed, axis=0), ng, wq, keys, u_packed, v_packed)
        xs = [y[offs[gi]:offs[gi + 1]].reshape(xs[gi].shape) for gi in range(len(xs))]
    ys = [pallas_rmsnorm(x, final_norm) for x in xs]
    return ys, new_states


def kernel(x_prompt, x_sample, cache_sb_k, cache_sb_v, state_ssd_conv, state_ssd,
           state_rwkv_shift, state_rwkv_wkv,
           l0_norm_mix, l0_w_in, l0_conv_w, l0_conv_b, l0_dt_bias, l0_a_log, l0_d_skip,
           l0_ssd_norm, l0_w_out,
           l1_norm_mix, l1_mu, l1_w_rkv, l1_w0, l1_w1, l1_w2, l1_a0, l1_a1, l1_a2,
           l1_g1, l1_g2, l1_k_k, l1_k_a, l1_r_k, l1_lnx_w, l1_lnx_b, l1_w_out,
           l0_norm_ffn, l0_peer_wq, l0_peer_keys, l0_peer_u, l0_peer_v,
           l1_norm_ffn, l1_peer_wq, l1_peer_keys, l1_peer_u, l1_peer_v,
           final_norm):
    layer_params = [
        (l0_norm_mix, l0_w_in, l0_conv_w, l0_conv_b, l0_dt_bias, l0_a_log, l0_d_skip,
         l0_ssd_norm, l0_w_out),
        (l1_norm_mix, l1_mu, l1_w_rkv, l1_w0, l1_w1, l1_w2, l1_a0, l1_a1, l1_a2,
         l1_g1, l1_g2, l1_k_k, l1_k_a, l1_r_k, l1_lnx_w, l1_lnx_b, l1_w_out),
    ]
    peer_params = [
        (l0_norm_ffn, l0_peer_wq, l0_peer_keys, pack_table(l0_peer_u), pack_table(l0_peer_v)),
        (l1_norm_ffn, l1_peer_wq, l1_peer_keys, pack_table(l1_peer_u), pack_table(l1_peer_v)),
    ]
    bp = x_prompt.shape[0]
    dt = x_prompt.dtype
    zero_states = [
        (jnp.zeros((bp, 0, SB_HEADS, SB_HEAD_DIM), dt), jnp.zeros((bp, 0, SB_HEADS, SB_HEAD_DIM), dt),
         jnp.zeros((bp, SSD_CONV - 1, SSD_CONV_DIM), dt),
         jnp.zeros((bp, SSD_HEADS, SSD_HEAD_DIM, SSD_STATE), dt)),
        (jnp.zeros((bp, 1, D_MODEL), dt), jnp.zeros((bp, RW_HEADS, RW_HEAD, RW_HEAD), dt)),
    ]
    sample_states = [
        (cache_sb_k, cache_sb_v, state_ssd_conv, state_ssd),
        (state_rwkv_shift, state_rwkv_wkv),
    ]
    (y_prompt, y_sample), (p_states, s_states) = run_trunks(
        [x_prompt, x_sample], [zero_states, sample_states], layer_params, peer_params, final_norm)
    p_sb_k, p_sb_v, p_conv, p_ssd, p_shift, p_wkv = p_states
    s_sb_k, s_sb_v, s_conv, s_ssd, s_shift, s_wkv = s_states
    return (y_prompt, y_sample, p_sb_k, p_sb_v, p_conv, p_ssd, p_shift, p_wkv,
            s_sb_k, s_sb_v, s_conv, s_ssd, s_shift, s_wkv)
```

```python
import functools
import math
import jax, jax.numpy as jnp
from jax import lax
import numpy as np
from jax.experimental import pallas as pl
from jax.experimental.pallas import tpu as pltpu

D_MODEL = 1024
DEPTH = 2

CHUNK = 64
EPS = 1e-6
SB_HEADS = 8
SB_HEAD_DIM = 64
SB_WIDTH = SB_HEADS * SB_HEAD_DIM
SB_BLOCK = 128
SSD_HEADS = 8
SSD_HEAD_DIM = 64
SSD_INNER = SSD_HEADS * SSD_HEAD_DIM
SSD_GROUPS = 2
SSD_STATE = 64
SSD_CONV = 4
SSD_CONV_DIM = SSD_INNER + 2 * SSD_GROUPS * SSD_STATE
SSD_CHUNK = CHUNK
HYB_SPLITS = (SB_WIDTH, 2 * SB_WIDTH, 3 * SB_WIDTH, 3 * SB_WIDTH + SSD_INNER,
              3 * SB_WIDTH + SSD_INNER + SSD_CONV_DIM)
HYB_IN = 3 * SB_WIDTH + SSD_INNER + SSD_CONV_DIM + SSD_HEADS
RW_HEAD = 64
RW_HEADS = D_MODEL // RW_HEAD
RW_LNX_EPS = 64e-5
PEER_HEADS = 8
PEER_NKEYS = 128
PEER_QDIM = 256
PEER_TOPK = 16
PEER_BLOCK = 128

F32 = jnp.float32
V7X_VMEM_BYTES = 64 * 1024 * 1024


def _cparams(*semantics):
    return pltpu.CompilerParams(dimension_semantics=semantics, vmem_limit_bytes=V7X_VMEM_BYTES)


def rmsnorm(x, g):
    xf = x.astype(F32)
    y = xf * lax.rsqrt(jnp.mean(xf * xf, axis=-1, keepdims=True) + EPS)
    return (y * g.astype(F32)).astype(x.dtype)


def _rmsnorm_kernel(x_ref, g_ref, o_ref):
    x = x_ref[...]
    y = x * lax.rsqrt(jnp.mean(x * x, axis=-1, keepdims=True) + EPS)
    o_ref[...] = y * g_ref[...]


def pallas_rmsnorm(x, g, rows=256):
    shp = x.shape
    x2 = x.reshape(-1, shp[-1])
    t, d = x2.shape
    rows = min(rows, t)
    out = pl.pallas_call(
        _rmsnorm_kernel,
        grid=(t // rows,),
        in_specs=[pl.BlockSpec((rows, d), lambda i: (i, 0)),
                  pl.BlockSpec((1, d), lambda i: (0, 0))],
        out_specs=pl.BlockSpec((rows, d), lambda i: (i, 0)),
        out_shape=jax.ShapeDtypeStruct((t, d), x.dtype),
        compiler_params=_cparams("parallel"),
        name="final_rmsnorm",
    )(x2, g.reshape(1, d))
    return out.reshape(shp)


SB_KBLOCK = 128
SB_LOG_FLOOR = -90.0


def _split3(x):
    h = x.astype(jnp.bfloat16)
    r = x - h.astype(F32)
    m = r.astype(jnp.bfloat16)
    l = (r - m.astype(F32)).astype(jnp.bfloat16)
    return h, m, l


def _sb_attn_kernel(q_ref, k_ref, v_ref, o_ref, *, tq, past):
    iq = pl.program_id(2)
    q = q_ref[0, 0]
    q_pos = past + iq * tq + lax.broadcasted_iota(jnp.int32, (tq, SB_KBLOCK), 0)
    k_off = lax.broadcasted_iota(jnp.int32, (tq, SB_KBLOCK), 1)
    r = lax.broadcasted_iota(jnp.int32, (SB_KBLOCK, SB_KBLOCK), 0)
    c = lax.broadcasted_iota(jnp.int32, (SB_KBLOCK, SB_KBLOCK), 1)
    upper = (r > c).astype(jnp.bfloat16)
    j0 = (past + (iq + 1) * tq - 1) // SB_KBLOCK

    def cond(state):
        j, live, _, _ = state
        return jnp.logical_and(j >= 0, live > 0)

    def body(state):
        j, _, carry, acc = state
        start = pl.multiple_of(j * SB_KBLOCK, SB_KBLOCK)
        kb = k_ref[0, 0, pl.ds(start, SB_KBLOCK), :]
        vb = v_ref[0, 0, pl.ds(start, SB_KBLOCK), :]
        z = lax.dot_general(q, kb, (((1,), (1,)), ((), ())), preferred_element_type=F32)
        mask = (j * SB_KBLOCK + k_off) < q_pos
        sp = jnp.maximum(z, 0.0) + jnp.log1p(jnp.exp(-jnp.abs(z)))
        log_fail = jnp.where(mask, -sp, 0.0)
        h, m, l = _split3(log_fail)
        later = (jnp.dot(h, upper, preferred_element_type=F32)
                 + jnp.dot(m, upper, preferred_element_type=F32)
                 + jnp.dot(l, upper, preferred_element_type=F32))
        log_w = (z - sp) + later + carry
        w = jnp.where(mask, jnp.exp(log_w), 0.0)
        acc = acc + jnp.dot(w.astype(jnp.bfloat16), vb, preferred_element_type=F32)
        carry = carry + jnp.sum(log_fail, axis=-1, keepdims=True)
        live = (jnp.max(carry) > SB_LOG_FLOOR).astype(jnp.int32)
        return j - 1, live, carry, acc

    state = (j0, jnp.int32(1), jnp.zeros((tq, 1), F32), jnp.zeros((tq, SB_HEAD_DIM), F32))
    _, _, _, acc = lax.while_loop(cond, body, state)
    o_ref[0, 0] = acc


def sb_attention(q, k_all, v_all, past):
    b, lq, h, d = q.shape
    s = k_all.shape[1]
    tq = min(lq, 128)
    s_pad = -(-s // SB_KBLOCK) * SB_KBLOCK
    qh = (q * (d ** -0.5)).astype(jnp.bfloat16).transpose(0, 2, 1, 3)
    kh = jnp.pad(k_all.astype(jnp.bfloat16).transpose(0, 2, 1, 3), ((0, 0), (0, 0), (0, s_pad - s), (0, 0)))
    vh = jnp.pad(v_all.astype(jnp.bfloat16).transpose(0, 2, 1, 3), ((0, 0), (0, 0), (0, s_pad - s), (0, 0)))
    out = pl.pallas_call(
        functools.partial(_sb_attn_kernel, tq=tq, past=past),
        grid=(b, h, lq // tq),
        in_specs=[pl.BlockSpec((1, 1, tq, d), lambda bi, hi, qi: (bi, hi, qi, 0)),
                  pl.BlockSpec((1, 1, s_pad, d), lambda bi, hi, qi: (bi, hi, 0, 0)),
                  pl.BlockSpec((1, 1, s_pad, d), lambda bi, hi, qi: (bi, hi, 0, 0))],
        out_specs=pl.BlockSpec((1, 1, tq, d), lambda bi, hi, qi: (bi, hi, qi, 0)),
        out_shape=jax.ShapeDtypeStruct((b, h, lq, d), F32),
        compiler_params=_cparams("parallel", "parallel", "arbitrary"),
        name="sb_attention",
    )(qh, kh, vh)
    return out.transpose(0, 2, 1, 3)


def causal_conv(u, buf, w, bias):
    L = u.shape[1]
    full = jnp.concatenate([buf, u], axis=1)
    out = bias
    for i in range(SSD_CONV):
        out = out + full[:, i:i + L] * w[i]
    return out, full[:, -(SSD_CONV - 1):]


def ssd_scan(x, dt, a, bm, cm, h0):
    b, L, H, P = x.shape
    Q = SSD_CHUNK if L % SSD_CHUNK == 0 else L
    nc = L // Q
    rep = H // SSD_GROUPS
    xf = x.astype(F32).reshape(b, nc, Q, H, P)
    bh = jnp.repeat(bm.astype(F32), rep, axis=2).reshape(b, nc, Q, H, SSD_STATE)
    ch = jnp.repeat(cm.astype(F32), rep, axis=2).reshape(b, nc, Q, H, SSD_STATE)
    dtc = dt.reshape(b, nc, Q, H)
    a_cum = jnp.cumsum(dtc * a, axis=2)
    seg = a_cum[:, :, :, None, :] - a_cum[:, :, None, :, :]
    causal = jnp.tril(jnp.ones((Q, Q), dtype=bool))[None, None, :, :, None]
    decay = jnp.exp(jnp.where(causal, seg, -jnp.inf))
    cb = jnp.einsum('bclhn,bcshn->bclsh', ch, bh)
    y_diag = jnp.einsum('bclsh,bcshp->bclhp', cb * decay * dtc[:, :, None], xf)
    w_state = jnp.exp(a_cum[:, :, -1:] - a_cum) * dtc
    chunk_states = jnp.einsum('bclhn,bclh,bclhp->bchpn', bh, w_state, xf)
    chunk_decay = jnp.exp(a_cum[:, :, -1])

    def step(h, inp):
        dec, st = inp
        return dec[:, :, None, None] * h + st, h

    h_last, h_prev = lax.scan(step, h0, (chunk_decay.transpose(1, 0, 2),
                                          chunk_states.transpose(1, 0, 2, 3, 4)))
    h_prev = h_prev.transpose(1, 0, 2, 3, 4)
    y_off = jnp.einsum('bclhn,bchpn->bclhp', ch, h_prev) * jnp.exp(a_cum)[..., None]
    return (y_diag + y_off).reshape(b, L, H, P), h_last


def hybrid_mixer(xn, k_cache, v_cache, conv_buf, ssd_h,
                 w_in, conv_w, conv_b, dt_bias, a_log, d_skip, ssd_norm, w_out):
    b, L, _ = xn.shape
    proj = xn @ w_in
    q, k, v, z, xbc, dt = jnp.split(proj, HYB_SPLITS, axis=-1)
    q = q.reshape(b, L, SB_HEADS, SB_HEAD_DIM)
    k = k.reshape(b, L, SB_HEADS, SB_HEAD_DIM)
    v = v.reshape(b, L, SB_HEADS, SB_HEAD_DIM)
    past = k_cache.shape[1]
    k_all = jnp.concatenate([k_cache, k], axis=1)
    v_all = jnp.concatenate([v_cache, v], axis=1)
    o_sb = sb_attention(q, k_all, v_all, past).reshape(b, L, SB_WIDTH)
    xbc_c, new_buf = causal_conv(xbc, conv_buf, conv_w, conv_b)
    xbc_c = jax.nn.silu(xbc_c)
    xs, bm, cm = jnp.split(xbc_c, (SSD_INNER, SSD_INNER + SSD_GROUPS * SSD_STATE), axis=-1)
    xs = xs.reshape(b, L, SSD_HEADS, SSD_HEAD_DIM)
    bm = bm.reshape(b, L, SSD_GROUPS, SSD_STATE)
    cm = cm.reshape(b, L, SSD_GROUPS, SSD_STATE)
    dtp = jax.nn.softplus(dt.astype(F32) + dt_bias.astype(F32))
    a = -jnp.exp(a_log.astype(F32))
    y, h_new = ssd_scan(xs, dtp, a, bm, cm, ssd_h.astype(F32))
    y = y + d_skip.astype(F32)[:, None] * xs.astype(F32)
    y = y.reshape(b, L, SSD_INNER) * jax.nn.silu(z.astype(F32))
    yg = y.reshape(b, L, SSD_GROUPS, SSD_INNER // SSD_GROUPS)
    yg = yg * lax.rsqrt(jnp.mean(yg * yg, axis=-1, keepdims=True) + EPS)
    y = yg.reshape(b, L, SSD_INNER) * ssd_norm.astype(F32)
    out = jnp.concatenate([o_sb, y.astype(xn.dtype)], axis=-1) @ w_out
    return out, k, v, new_buf, h_new.astype(ssd_h.dtype)


RW_PAIRS = RW_HEADS // 2
RW_NP = 4
RW_TC = 64


def _rwkv_scan_kernel(r_ref, d_ref, k_ref, v_ref, a_ref, b_ref, s0_ref, o_ref, st_ref, s_scr, vcol_scr, *, tc):
    c = pl.program_id(2)

    @pl.when(c == 0)
    def _():
        s_scr[...] = s0_ref[0]

    row = lax.broadcasted_iota(jnp.int32, (RW_HEAD, 128), 0)
    lane = lax.broadcasted_iota(jnp.int32, (RW_HEAD, 128), 1)
    left = lane < RW_HEAD
    diag = (lane & (RW_HEAD - 1)) == row
    same_head = ((lax.broadcasted_iota(jnp.int32, (128, 128), 0) < RW_HEAD)
                 == (lax.broadcasted_iota(jnp.int32, (128, 128), 1) < RW_HEAD)).astype(jnp.bfloat16)
    same_head3 = jnp.concatenate([same_head] * 3, axis=0)

    def half_sums(x):
        lo = jnp.sum(jnp.where(left, x, 0.0), axis=1, keepdims=True)
        hi = jnp.sum(jnp.where(left, 0.0, x), axis=1, keepdims=True)
        return jnp.where(left, lo, hi)

    def vcol_body(t8, carry):
        base = pl.multiple_of(t8 * 8, 8)
        for p in range(RW_NP):
            terms = _terms3(v_ref[0, pl.ds(base, 8), p * 128:(p + 1) * 128])
            for i in range(8):
                h, m, l = (jnp.where(diag, t[i:i + 1], 0.0).astype(jnp.bfloat16) for t in terms)
                vcol_scr[p, base + i] = jnp.dot(jnp.concatenate([h, m, l], axis=1), same_head3,
                                                preferred_element_type=F32)
        return carry

    lax.fori_loop(0, tc // 8, vcol_body, 0)

    def step8(t8, carry):
        base = pl.multiple_of(t8 * 8, 8)
        lanes = [slice(p * 128, (p + 1) * 128) for p in range(RW_NP)]
        blk = [[ref[0, pl.ds(base, 8), sl] for ref in (a_ref, d_ref, b_ref, k_ref, r_ref)] for sl in lanes]
        state = [s_scr[p] for p in range(RW_NP)]
        rows = [[] for _ in range(RW_NP)]
        for i in range(8):
            for p in range(RW_NP):
                a8, d8, b8, k8, r8 = blk[p]
                s = state[p]
                sa = half_sums(s * a8[i:i + 1])
                s = s * d8[i:i + 1] + sa * b8[i:i + 1] + vcol_scr[p, base + i] * k8[i:i + 1]
                state[p] = s
                h, m, l = _split3(s * r8[i:i + 1])
                oc = jnp.dot(jnp.concatenate([h, m, l], axis=1), same_head3, preferred_element_type=F32)
                rows[p].append(jnp.sum(jnp.where(diag, oc, 0.0), axis=0, keepdims=True))
        for p in range(RW_NP):
            s_scr[p] = state[p]
            o_ref[0, pl.ds(base, 8), lanes[p]] = jnp.concatenate(rows[p], axis=0)
        return carry

    lax.fori_loop(0, tc // 8, step8, 0)

    @pl.when(c == pl.num_programs(2) - 1)
    def _():
        st_ref[0] = s_scr[...]


def rwkv_scan(r, d, k, v, a, b, wkv):
    bsz, L, D = r.shape
    tc = min(RW_TC, L)
    s0 = wkv.reshape(bsz, RW_PAIRS, 2, RW_HEAD, RW_HEAD).transpose(0, 1, 3, 2, 4).reshape(bsz, RW_PAIRS, RW_HEAD, 128)
    seq = pl.BlockSpec((1, tc, RW_NP * 128), lambda bi, g, c: (bi, c, g))
    st = pl.BlockSpec((1, RW_NP, RW_HEAD, 128), lambda bi, g, c: (bi, g, 0, 0))
    o, s_new = pl.pallas_call(
        functools.partial(_rwkv_scan_kernel, tc=tc),
        grid=(bsz, RW_PAIRS // RW_NP, L // tc),
        in_specs=[seq] * 6 + [st],
        out_specs=[seq, st],
        out_shape=[jax.ShapeDtypeStruct((bsz, L, D), F32),
                   jax.ShapeDtypeStruct((bsz, RW_PAIRS, RW_HEAD, 128), F32)],
        scratch_shapes=[pltpu.VMEM((RW_NP, RW_HEAD, 128), F32),
                        pltpu.VMEM((RW_NP, tc, RW_HEAD, 128), F32)],
        compiler_params=_cparams("parallel", "parallel", "arbitrary"),
        name="rwkv_scan",
    )(r, d, k, v, a, b, s0)
    s_new = s_new.reshape(bsz, RW_PAIRS, RW_HEAD, 2, RW_HEAD).transpose(0, 1, 3, 2, 4).reshape(bsz, RW_HEADS, RW_HEAD, RW_HEAD)
    return o, s_new


def _proj3_kernel(x_ref, w_ref, o_ref):
    o_ref[0] = jnp.dot(x_ref[0], w_ref[0], preferred_element_type=F32)


def proj3(x3, w3):
    _, m, kdim = x3.shape
    n = w3.shape[2]
    tm = min(m, 512)
    tn = min(n, 512)
    return pl.pallas_call(
        _proj3_kernel,
        grid=(3, m // tm, n // tn),
        in_specs=[pl.BlockSpec((1, tm, kdim), lambda i, a, b: (i, a, 0)),
                  pl.BlockSpec((1, kdim, tn), lambda i, a, b: (i, 0, b))],
        out_specs=pl.BlockSpec((1, tm, tn), lambda i, a, b: (i, a, b)),
        out_shape=jax.ShapeDtypeStruct((3, m, n), F32),
        compiler_params=_cparams("parallel", "parallel", "parallel"),
        name="rwkv_proj3",
    )(x3, w3)


def rwkv7_mixer(xn, shift_buf, wkv, mu, w_rkv, w0, w1, w2, a0, a1, a2, g1, g2,
                k_k, k_a, r_k, lnx_w, lnx_b, w_out):
    b, L, D = xn.shape
    x_prev = jnp.concatenate([shift_buf, xn[:, :-1]], axis=1)
    xx = x_prev - xn
    xm = xn[None] + xx[None] * mu[:, None, None, :]
    r, k, v = proj3(xm[:3].astype(jnp.bfloat16).reshape(3, b * L, D),
                    w_rkv.astype(jnp.bfloat16)).reshape(3, b, L, D)
    xw, xa, xg = xm[3], xm[4], xm[5]
    w = -jax.nn.softplus(-(w0 + jnp.tanh(xw @ w1) @ w2).astype(F32)) - 0.5
    decay = jnp.exp(-jnp.exp(w))
    a = jax.nn.sigmoid((a0 + (xa @ a1) @ a2).astype(F32))
    g = (jax.nn.sigmoid(xg @ g1) @ g2).astype(F32)
    hs = (b, L, RW_HEADS, RW_HEAD)
    kk = (k * k_k.astype(F32)).reshape(hs)
    kk = kk / jnp.maximum(jnp.sqrt(jnp.sum(kk * kk, axis=-1, keepdims=True)), 1e-12)
    k = k * (1.0 + (a - 1.0) * k_a.astype(F32))
    rh, dh, kh, vh, ah = (t.reshape(hs) for t in (r, decay, k, v, a))
    a_vec = -kk
    b_vec = kk * ah
    o, S_last = rwkv_scan(r, decay, k, v, a_vec.reshape(b, L, D), b_vec.reshape(b, L, D), wkv.astype(F32))
    o = o.reshape(hs)
    mean = jnp.mean(o, axis=-1, keepdims=True)
    var = jnp.mean(jnp.square(o - mean), axis=-1, keepdims=True)
    o = ((o - mean) * lax.rsqrt(var + RW_LNX_EPS)).reshape(b, L, D)
    o = o * lnx_w.astype(F32) + lnx_b.astype(F32)
    bonus = jnp.sum(rh * kh * r_k.astype(F32), axis=-1, keepdims=True) * vh
    o = o + bonus.reshape(b, L, D)
    out = (o * g).astype(xn.dtype) @ w_out
    return out, xn[:, -1:], S_last.astype(wkv.dtype)


PEER_TM = 256
PEER_TB = 32
PEER_NSTAGE = 4
PEER_HALF = D_MODEL // 2
PEER_SUB = PEER_HALF // 128
PEER_SEL = PEER_HEADS * PEER_TOPK


def _peer_scores_kernel(x_ref, g_ref, wq_ref, keys_ref, s_ref, xn_ref, xnb_ref):
    @pl.when(pl.program_id(1) == 0)
    def _():
        x = x_ref[...]
        xn = x * lax.rsqrt(jnp.mean(x * x, axis=-1, keepdims=True) + EPS) * g_ref[...]
        xn_ref[...] = xn
        xnb_ref[...] = xn.astype(jnp.bfloat16)

    q = jnp.dot(xnb_ref[...], wq_ref[...], preferred_element_type=F32).astype(jnp.bfloat16)
    half = PEER_QDIM // 2
    for i in range(2):
        s_ref[i * PEER_NKEYS:(i + 1) * PEER_NKEYS, :] = lax.dot_general(
            keys_ref[0, i], q[:, i * half:(i + 1) * half],
            (((1,), (1,)), ((), ())), preferred_element_type=F32)


def peer_scores(x, g, wq_bf, keys_bf):
    t, d = x.shape
    return pl.pallas_call(
        _peer_scores_kernel,
        grid=(t // PEER_TM, PEER_HEADS),
        in_specs=[pl.BlockSpec((PEER_TM, d), lambda i, h: (i, 0)),
                  pl.BlockSpec((1, d), lambda i, h: (0, 0)),
                  pl.BlockSpec((d, PEER_QDIM), lambda i, h: (0, h)),
                  pl.BlockSpec((1, 2, PEER_NKEYS, PEER_QDIM // 2), lambda i, h: (h, 0, 0, 0))],
        out_specs=[pl.BlockSpec((2 * PEER_NKEYS, PEER_TM), lambda i, h: (h, i)),
                   pl.BlockSpec((PEER_TM, d), lambda i, h: (i, 0))],
        out_shape=[jax.ShapeDtypeStruct((PEER_HEADS * 2 * PEER_NKEYS, t), F32),
                   jax.ShapeDtypeStruct((t, d), F32)],
        scratch_shapes=[pltpu.VMEM((PEER_TM, d), jnp.bfloat16)],
        compiler_params=_cparams("parallel", "arbitrary"),
        name="peer_scores",
    )(x, g.reshape(1, d), wq_bf, keys_bf)


PEER_TT = 128
NEG_INF = float("-inf")


def _peer_candidates():
    k = PEER_TOPK
    a_idx, b_idx, valid = [], [], []
    for a in range(k // 2):
        nb = k // (a + 1)
        rows = k if a == 0 else 8
        for b in range(rows):
            a_idx.append(a); b_idx.append(b); valid.append(b < nb)
    for a in range(k // 2, k):
        a_idx.append(a); b_idx.append(0); valid.append(True)
    return np.array(a_idx), np.array(b_idx), np.array(valid)


_CAND_A, _CAND_B, _CAND_VALID = _peer_candidates()
PEER_NCAND = len(_CAND_A)


def _top_rows(x, tag, k):
    big = jnp.float32(1e9)
    vals, tags = [], []
    for _ in range(k):
        m = jnp.max(x, axis=0, keepdims=True)
        t = jnp.min(jnp.where(x == m, tag, big), axis=0, keepdims=True)
        vals.append(m)
        tags.append(t)
        x = jnp.where(tag == t, NEG_INF, x)
    return vals, tags


def _peer_topk_kernel(s_ref, pos_ref, eid_ref, gate_ref, v_scr, i_scr, e_scr, g_scr):
    k = PEER_TOPK
    key_tag = lax.broadcasted_iota(jnp.int32, (PEER_NKEYS, PEER_TT), 0).astype(F32)

    def half_body(g, carry):
        start = pl.multiple_of(g * PEER_NKEYS, PEER_NKEYS)
        vals, tags = _top_rows(s_ref[pl.ds(start, PEER_NKEYS), :], key_tag, k)
        v_scr[g] = jnp.concatenate(vals, axis=0)
        i_scr[g] = jnp.concatenate(tags, axis=0)
        return carry

    lax.fori_loop(0, 2 * PEER_HEADS, half_body, 0)

    pos = pos_ref[...]
    valid = pos >= 0.0

    def head_body(h, carry):
        s1 = v_scr[2 * h]; s2 = v_scr[2 * h + 1]
        i1 = i_scr[2 * h]; i2 = i_scr[2 * h + 1]
        e2 = i2[0:8]
        pieces_s = [s1[0:1] + s2]
        pieces_e = [i1[0:1] * PEER_NKEYS + i2]
        for a in range(1, k // 2):
            pieces_s.append(s1[a:a + 1] + s2[0:8])
            pieces_e.append(i1[a:a + 1] * PEER_NKEYS + e2)
        pieces_s.append(s1[k // 2:k] + s2[0:1])
        pieces_e.append(i1[k // 2:k] * PEER_NKEYS + i2[0:1])
        cand = jnp.where(valid, jnp.concatenate(pieces_s, axis=0), NEG_INF)
        cid = jnp.concatenate(pieces_e, axis=0)
        tops, tags = _top_rows(cand, pos, k)
        eids = [jnp.max(jnp.where(pos == t, cid, -1.0), axis=0, keepdims=True) for t in tags]
        top = jnp.concatenate(tops, axis=0)
        p = jnp.exp(top - top[0:1])
        gate = p / jnp.sum(p, axis=0, keepdims=True)
        row = pl.multiple_of(h * k, k)
        e_scr[pl.ds(row, k), :] = jnp.concatenate(eids, axis=0)
        g_scr[pl.ds(row, k), :] = gate
        return carry

    lax.fori_loop(0, PEER_HEADS, head_body, 0)
    eid_ref[...] = e_scr[...].T
    gate_ref[...] = g_scr[...].T


def peer_topk(scores_t):
    t = scores_t.shape[1]
    pos = np.where(_CAND_VALID, _CAND_A * PEER_TOPK + _CAND_B, -1).astype(np.float32)
    pos = jnp.asarray(np.broadcast_to(pos[:, None], (PEER_NCAND, PEER_TT)))
    return pl.pallas_call(
        _peer_topk_kernel,
        grid=(t // PEER_TT,),
        in_specs=[pl.BlockSpec((2 * PEER_HEADS * PEER_NKEYS, PEER_TT), lambda i: (0, i)),
                  pl.BlockSpec((PEER_NCAND, PEER_TT), lambda i: (0, 0))],
        out_specs=[pl.BlockSpec((PEER_TT, PEER_SEL), lambda i: (i, 0)),
                   pl.BlockSpec((PEER_TT, PEER_SEL), lambda i: (i, 0))],
        out_shape=[jax.ShapeDtypeStruct((t, PEER_SEL), F32),
                   jax.ShapeDtypeStruct((t, PEER_SEL), F32)],
        scratch_shapes=[pltpu.VMEM((2 * PEER_HEADS, PEER_TOPK, PEER_TT), F32),
                        pltpu.VMEM((2 * PEER_HEADS, PEER_TOPK, PEER_TT), F32),
                        pltpu.VMEM((PEER_SEL, PEER_TT), F32),
                        pltpu.VMEM((PEER_SEL, PEER_TT), F32)],
        compiler_params=_cparams("parallel"),
        name="peer_topk",
    )(scores_t, pos)


def pack_table(tab):
    e = tab.shape[0]
    b = lax.bitcast_convert_type(tab.astype(jnp.bfloat16), jnp.uint16).astype(jnp.uint32)
    b = b.reshape(e, 2, PEER_SUB, 128)
    return (b[:, 0] << 16) | b[:, 1]


def _terms3(x):
    h = x.astype(jnp.bfloat16).astype(F32)
    r = x - h
    m = r.astype(jnp.bfloat16).astype(F32)
    l = (r - m).astype(jnp.bfloat16).astype(F32)
    return h, m, l


def _stack_terms(terms, s):
    row = lax.broadcasted_iota(jnp.int32, (8, 128), 0)
    h, m, l = (t[s:s + 1] for t in terms)
    return jnp.where(row == 0, h, jnp.where(row == 1, m, jnp.where(row == 2, l, 0.0))).astype(jnp.bfloat16)


def _stage_rows(eid_ref, tab_ref, w_scr, t):
    for j in range(PEER_SEL):
        w_scr[PEER_SUB * j:PEER_SUB * (j + 1), :] = tab_ref[eid_ref[t, j]]


def _token_groups(eid_ref, tab_ref, w_scrs, compute):
    n = len(w_scrs)

    def group(i, carry):
        for k in range(n):
            _stage_rows(eid_ref, tab_ref, w_scrs[k], n * i + k)
        for k in range(n):
            compute(n * i + k, w_scrs[k])
        return carry

    lax.fori_loop(0, PEER_TB // n, group, 0)


def _halves(w_scr, s):
    w = w_scr[pl.ds(s, PEER_SEL, stride=PEER_SUB), :]
    hi = lax.bitcast_convert_type(w & jnp.uint32(0xFFFF0000), F32).astype(jnp.bfloat16)
    lo = lax.bitcast_convert_type(w << 16, F32).astype(jnp.bfloat16)
    return hi, lo


def _peer_hidden_kernel(eid_ref, x_ref, tab_ref, hid_ref, *w_scrs):
    nt = (((1,), (1,)), ((), ()))

    def compute(t, w_scr):
        th = _terms3(x_ref[t, 0])
        tl = _terms3(x_ref[t, 1])
        acc = jnp.zeros((8, PEER_SEL), F32)
        for s in range(PEER_SUB):
            hi, lo = _halves(w_scr, s)
            acc = acc + lax.dot_general(_stack_terms(th, s), hi, nt, preferred_element_type=F32)
            acc = acc + lax.dot_general(_stack_terms(tl, s), lo, nt, preferred_element_type=F32)
        hid_ref[t] = acc[0:1] + acc[1:2] + acc[2:3]

    _token_groups(eid_ref, tab_ref, w_scrs, compute)


def _peer_out_kernel(eid_ref, act_ref, res_ref, tab_ref, out_ref, *w_scrs):
    def compute(t, w_scr):
        a8 = _stack_terms(_terms3(act_ref[t]), 0)
        for s in range(PEER_SUB):
            hi, lo = _halves(w_scr, s)
            yh = jnp.dot(a8, hi, preferred_element_type=F32)
            yl = jnp.dot(a8, lo, preferred_element_type=F32)
            out_ref[t, 0, s:s + 1, :] = res_ref[t, 0, s:s + 1, :] + (yh[0:1] + yh[1:2] + yh[2:3])
            out_ref[t, 1, s:s + 1, :] = res_ref[t, 1, s:s + 1, :] + (yl[0:1] + yl[1:2] + yl[2:3])

    _token_groups(eid_ref, tab_ref, w_scrs, compute)


def _table_spec(n_exp):
    return pl.BlockSpec((n_exp, PEER_SUB, 128), lambda i: (0, 0, 0), pipeline_mode=pl.Buffered(1))


def peer_hidden(eid, xn_split, tab_packed):
    t = eid.shape[0]
    n_exp = tab_packed.shape[0]
    return pl.pallas_call(
        _peer_hidden_kernel,
        grid=(t // PEER_TB,),
        in_specs=[pl.BlockSpec((PEER_TB, PEER_SEL), lambda i: (i, 0), memory_space=pltpu.SMEM),
                  pl.BlockSpec((PEER_TB, 2, PEER_SUB, 128), lambda i: (i, 0, 0, 0)),
                  _table_spec(n_exp)],
        out_specs=pl.BlockSpec((PEER_TB, 1, PEER_SEL), lambda i: (i, 0, 0)),
        out_shape=jax.ShapeDtypeStruct((t, 1, PEER_SEL), F32),
        scratch_shapes=[pltpu.VMEM((PEER_SEL * PEER_SUB, 128), jnp.uint32)] * PEER_NSTAGE,
        compiler_params=_cparams("parallel"),
        name="peer_hidden",
    )(eid, xn_split, tab_packed)


def peer_out(eid, act, res_split, tab_packed):
    t = eid.shape[0]
    n_exp = tab_packed.shape[0]
    return pl.pallas_call(
        _peer_out_kernel,
        grid=(t // PEER_TB,),
        in_specs=[pl.BlockSpec((PEER_TB, PEER_SEL), lambda i: (i, 0), memory_space=pltpu.SMEM),
                  pl.BlockSpec((PEER_TB, 1, PEER_SEL), lambda i: (i, 0, 0)),
                  pl.BlockSpec((PEER_TB, 2, PEER_SUB, 128), lambda i: (i, 0, 0, 0)),
                  _table_spec(n_exp)],
        out_specs=pl.BlockSpec((PEER_TB, 2, PEER_SUB, 128), lambda i: (i, 0, 0, 0)),
        out_shape=jax.ShapeDtypeStruct((t, 2, PEER_SUB, 128), F32),
        scratch_shapes=[pltpu.VMEM((PEER_SEL * PEER_SUB, 128), jnp.uint32)] * PEER_NSTAGE,
        compiler_params=_cparams("parallel"),
        name="peer_out",
    )(eid, act.reshape(t, 1, PEER_SEL), res_split, tab_packed)


def peer_layer(x, ng, w_q, keys, u_packed, v_packed):
    t, d = x.shape
    scores, xn = peer_scores(x, ng, w_q.astype(jnp.bfloat16), keys.astype(jnp.bfloat16))
    eid_f, gate = peer_topk(scores)
    eid = eid_f.astype(jnp.int32)
    hidden = peer_hidden(eid, xn.reshape(t, 2, PEER_SUB, 128), u_packed).reshape(t, PEER_SEL)
    act = jax.nn.gelu(hidden, approximate=False) * gate
    out = peer_out(eid, act, x.reshape(t, 2, PEER_SUB, 128), v_packed)
    return out.reshape(t, d)


def mixer_layer(l, x, states, mp):
    xn = rmsnorm(x, mp[0])
    if l % 2 == 0:
        out, *ns = hybrid_mixer(xn, *states, *mp[1:])
    else:
        out, *ns = rwkv7_mixer(xn, *states, *mp[1:])
    return x + out, ns


def run_trunks(xs, states_list, layer_params, peer_params, final_norm):
    new_states = [[] for _ in xs]
    sizes = [x.shape[0] * x.shape[1] for x in xs]
    offs = np.cumsum([0] + sizes)
    for l in range(DEPTH):
        mixed = []
        for gi, x in enumerate(xs):
            x, ns = mixer_layer(l, x, states_list[gi][l], layer_params[l])
            new_states[gi].extend(ns)
            mixed.append(x.reshape(-1, D_MODEL))
        ng, wq, keys, u_packed, v_packed = peer_params[l]
        y = peer_layer(jnp.concatenate(mixed, axis=0), ng, wq, keys, u_packed, v_packed)
        xs = [y[offs[gi]:offs[gi + 1]].reshape(xs[gi].shape) for gi in range(len(xs))]
    ys = [pallas_rmsnorm(x, final_norm) for x in xs]
    return ys, new_states


def kernel(x_prompt, x_sample, cache_sb_k, cache_sb_v, state_ssd_conv, state_ssd,
           state_rwkv_shift, state_rwkv_wkv,
           l0_norm_mix, l0_w_in, l0_conv_w, l0_conv_b, l0_dt_bias, l0_a_log, l0_d_skip,
           l0_ssd_norm, l0_w_out,
           l1_norm_mix, l1_mu, l1_w_rkv, l1_w0, l1_w1, l1_w2, l1_a0, l1_a1, l1_a2,
           l1_g1, l1_g2, l1_k_k, l1_k_a, l1_r_k, l1_lnx_w, l1_lnx_b, l1_w_out,
           l0_norm_ffn, l0_peer_wq, l0_peer_keys, l0_peer_u, l0_peer_v,
           l1_norm_ffn, l1_peer_wq, l1_peer_keys, l1_peer_u, l1_peer_v,
           final_norm):
    layer_params = [
        (l0_norm_mix, l0_w_in, l0_conv_w, l0_conv_b, l0_dt_bias, l0_a_log, l0_d_skip,
         l0_ssd_norm, l0_w_out),
        (l1_norm_mix, l1_mu, l1_w_rkv, l1_w0, l1_w1, l1_w2, l1_a0, l1_a1, l1_a2,
         l1_g1, l1_g2, l1_k_k, l1_k_a, l1_r_k, l1_lnx_w, l1_lnx_b, l1_w_out),
    ]
    peer_params = [
        (l0_norm_ffn, l0_peer_wq, l0_peer_keys, pack_table(l0_peer_u), pack_table(l0_peer_v)),
        (l1_norm_ffn, l1_peer_wq, l1_peer_keys, pack_table(l1_peer_u), pack_table(l1_peer_v)),
    ]
    bp = x_prompt.shape[0]
    dt = x_prompt.dtype
    zero_states = [
        (jnp.zeros((bp, 0, SB_HEADS, SB_HEAD_DIM), dt), jnp.zeros((bp, 0, SB_HEADS, SB_HEAD_DIM), dt),
         jnp.zeros((bp, SSD_CONV - 1, SSD_CONV_DIM), dt),
         jnp.zeros((bp, SSD_HEADS, SSD_HEAD_DIM, SSD_STATE), dt)),
        (jnp.zeros((bp, 1, D_MODEL), dt), jnp.zeros((bp, RW_HEADS, RW_HEAD, RW_HEAD), dt)),
    ]
    sample_states = [
        (cache_sb_k, cache_sb_v, state_ssd_conv, state_ssd),
        (state_rwkv_shift, state_rwkv_wkv),
    ]
    (y_prompt, y_sample), (p_states, s_states) = run_trunks(
        [x_prompt, x_sample], [zero_states, sample_states], layer_params, peer_params, final_norm)
    p_sb_k, p_sb_v, p_conv, p_ssd, p_shift, p_wkv = p_states
    s_sb_k, s_sb_v, s_conv, s_ssd, s_shift, s_wkv = s_states
    return (y_prompt, y_sample, p_sb_k, p_sb_v, p_conv, p_ssd, p_shift, p_wkv,
            s_sb_k, s_sb_v, s_conv, s_ssd, s_shift, s_wkv)
```

```python
import functools
import math
import jax, jax.numpy as jnp
from jax import lax
import numpy as np
from jax.experimental import pallas as pl
from jax.experimental.pallas import tpu as pltpu

D_MODEL = 1024
DEPTH = 2

CHUNK = 64
EPS = 1e-6
SB_HEADS = 8
SB_HEAD_DIM = 64
SB_WIDTH = SB_HEADS * SB_HEAD_DIM
SB_BLOCK = 128
SSD_HEADS = 8
SSD_HEAD_DIM = 64
SSD_INNER = SSD_HEADS * SSD_HEAD_DIM
SSD_GROUPS = 2
SSD_STATE = 64
SSD_CONV = 4
SSD_CONV_DIM = SSD_INNER + 2 * SSD_GROUPS * SSD_STATE
SSD_CHUNK = CHUNK
HYB_SPLITS = (SB_WIDTH, 2 * SB_WIDTH, 3 * SB_WIDTH, 3 * SB_WIDTH + SSD_INNER,
              3 * SB_WIDTH + SSD_INNER + SSD_CONV_DIM)
HYB_IN = 3 * SB_WIDTH + SSD_INNER + SSD_CONV_DIM + SSD_HEADS
RW_HEAD = 64
RW_HEADS = D_MODEL // RW_HEAD
RW_LNX_EPS = 64e-5
PEER_HEADS = 8
PEER_NKEYS = 128
PEER_QDIM = 256
PEER_TOPK = 16
PEER_BLOCK = 128

F32 = jnp.float32
V7X_VMEM_BYTES = 64 * 1024 * 1024


def _cparams(*semantics):
    return pltpu.CompilerParams(dimension_semantics=semantics, vmem_limit_bytes=V7X_VMEM_BYTES)


def rmsnorm(x, g):
    xf = x.astype(F32)
    y = xf * lax.rsqrt(jnp.mean(xf * xf, axis=-1, keepdims=True) + EPS)
    return (y * g.astype(F32)).astype(x.dtype)


def _rmsnorm_kernel(x_ref, g_ref, o_ref):
    x = x_ref[...]
    y = x * lax.rsqrt(jnp.mean(x * x, axis=-1, keepdims=True) + EPS)
    o_ref[...] = y * g_ref[...]


def pallas_rmsnorm(x, g, rows=256):
    shp = x.shape
    x2 = x.reshape(-1, shp[-1])
    t, d = x2.shape
    rows = min(rows, t)
    out = pl.pallas_call(
        _rmsnorm_kernel,
        grid=(t // rows,),
        in_specs=[pl.BlockSpec((rows, d), lambda i: (i, 0)),
                  pl.BlockSpec((1, d), lambda i: (0, 0))],
        out_specs=pl.BlockSpec((rows, d), lambda i: (i, 0)),
        out_shape=jax.ShapeDtypeStruct((t, d), x.dtype),
        compiler_params=_cparams("parallel"),
        name="final_rmsnorm",
    )(x2, g.reshape(1, d))
    return out.reshape(shp)


SB_KBLOCK = 128
SB_LOG_FLOOR = -90.0


def _split3(x):
    h = x.astype(jnp.bfloat16)
    r = x - h.astype(F32)
    m = r.astype(jnp.bfloat16)
    l = (r - m.astype(F32)).astype(jnp.bfloat16)
    return h, m, l


def _sb_attn_kernel(q_ref, k_ref, v_ref, o_ref, *, tq, past):
    iq = pl.program_id(2)
    q = q_ref[0, 0]
    q_pos = past + iq * tq + lax.broadcasted_iota(jnp.int32, (tq, SB_KBLOCK), 0)
    k_off = lax.broadcasted_iota(jnp.int32, (tq, SB_KBLOCK), 1)
    r = lax.broadcasted_iota(jnp.int32, (SB_KBLOCK, SB_KBLOCK), 0)
    c = lax.broadcasted_iota(jnp.int32, (SB_KBLOCK, SB_KBLOCK), 1)
    upper = (r > c).astype(jnp.bfloat16)
    j0 = (past + (iq + 1) * tq - 1) // SB_KBLOCK

    def cond(state):
        j, live, _, _ = state
        return jnp.logical_and(j >= 0, live > 0)

    def body(state):
        j, _, carry, acc = state
        start = pl.multiple_of(j * SB_KBLOCK, SB_KBLOCK)
        kb = k_ref[0, 0, pl.ds(start, SB_KBLOCK), :]
        vb = v_ref[0, 0, pl.ds(start, SB_KBLOCK), :]
        z = lax.dot_general(q, kb, (((1,), (1,)), ((), ())), preferred_element_type=F32)
        mask = (j * SB_KBLOCK + k_off) < q_pos
        sp = jnp.maximum(z, 0.0) + jnp.log1p(jnp.exp(-jnp.abs(z)))
        log_fail = jnp.where(mask, -sp, 0.0)
        h, m, l = _split3(log_fail)
        later = (jnp.dot(h, upper, preferred_element_type=F32)
                 + jnp.dot(m, upper, preferred_element_type=F32)
                 + jnp.dot(l, upper, preferred_element_type=F32))
        log_w = (z - sp) + later + carry
        w = jnp.where(mask, jnp.exp(log_w), 0.0)
        acc = acc + jnp.dot(w.astype(jnp.bfloat16), vb, preferred_element_type=F32)
        carry = carry + jnp.sum(log_fail, axis=-1, keepdims=True)
        live = (jnp.max(carry) > SB_LOG_FLOOR).astype(jnp.int32)
        return j - 1, live, carry, acc

    state = (j0, jnp.int32(1), jnp.zeros((tq, 1), F32), jnp.zeros((tq, SB_HEAD_DIM), F32))
    _, _, _, acc = lax.while_loop(cond, body, state)
    o_ref[0, 0] = acc


def sb_attention(q, k_all, v_all, past):
    b, lq, h, d = q.shape
    s = k_all.shape[1]
    tq = min(lq, 128)
    s_pad = -(-s // SB_KBLOCK) * SB_KBLOCK
    qh = (q * (d ** -0.5)).astype(jnp.bfloat16).transpose(0, 2, 1, 3)
    kh = jnp.pad(k_all.astype(jnp.bfloat16).transpose(0, 2, 1, 3), ((0, 0), (0, 0), (0, s_pad - s), (0, 0)))
    vh = jnp.pad(v_all.astype(jnp.bfloat16).transpose(0, 2, 1, 3), ((0, 0), (0, 0), (0, s_pad - s), (0, 0)))
    out = pl.pallas_call(
        functools.partial(_sb_attn_kernel, tq=tq, past=past),
        grid=(b, h, lq // tq),
        in_specs=[pl.BlockSpec((1, 1, tq, d), lambda bi, hi, qi: (bi, hi, qi, 0)),
                  pl.BlockSpec((1, 1, s_pad, d), lambda bi, hi, qi: (bi, hi, 0, 0)),
                  pl.BlockSpec((1, 1, s_pad, d), lambda bi, hi, qi: (bi, hi, 0, 0))],
        out_specs=pl.BlockSpec((1, 1, tq, d), lambda bi, hi, qi: (bi, hi, qi, 0)),
        out_shape=jax.ShapeDtypeStruct((b, h, lq, d), F32),
        compiler_params=_cparams("parallel", "parallel", "arbitrary"),
        name="sb_attention",
    )(qh, kh, vh)
    return out.transpose(0, 2, 1, 3)


def causal_conv(u, buf, w, bias):
    L = u.shape[1]
    full = jnp.concatenate([buf, u], axis=1)
    out = bias
    for i in range(SSD_CONV):
        out = out + full[:, i:i + L] * w[i]
    return out, full[:, -(SSD_CONV - 1):]


def ssd_scan(x, dt, a, bm, cm, h0):
    b, L, H, P = x.shape
    Q = SSD_CHUNK if L % SSD_CHUNK == 0 else L
    nc = L // Q
    rep = H // SSD_GROUPS
    xf = x.astype(F32).reshape(b, nc, Q, H, P)
    bh = jnp.repeat(bm.astype(F32), rep, axis=2).reshape(b, nc, Q, H, SSD_STATE)
    ch = jnp.repeat(cm.astype(F32), rep, axis=2).reshape(b, nc, Q, H, SSD_STATE)
    dtc = dt.reshape(b, nc, Q, H)
    a_cum = jnp.cumsum(dtc * a, axis=2)
    seg = a_cum[:, :, :, None, :] - a_cum[:, :, None, :, :]
    causal = jnp.tril(jnp.ones((Q, Q), dtype=bool))[None, None, :, :, None]
    decay = jnp.exp(jnp.where(causal, seg, -jnp.inf))
    cb = jnp.einsum('bclhn,bcshn->bclsh', ch, bh)
    y_diag = jnp.einsum('bclsh,bcshp->bclhp', cb * decay * dtc[:, :, None], xf)
    w_state = jnp.exp(a_cum[:, :, -1:] - a_cum) * dtc
    chunk_states = jnp.einsum('bclhn,bclh,bclhp->bchpn', bh, w_state, xf)
    chunk_decay = jnp.exp(a_cum[:, :, -1])

    def step(h, inp):
        dec, st = inp
        return dec[:, :, None, None] * h + st, h

    h_last, h_prev = lax.scan(step, h0, (chunk_decay.transpose(1, 0, 2),
                                          chunk_states.transpose(1, 0, 2, 3, 4)))
    h_prev = h_prev.transpose(1, 0, 2, 3, 4)
    y_off = jnp.einsum('bclhn,bchpn->bclhp', ch, h_prev) * jnp.exp(a_cum)[..., None]
    return (y_diag + y_off).reshape(b, L, H, P), h_last


def hybrid_mixer(xn, k_cache, v_cache, conv_buf, ssd_h,
                 w_in, conv_w, conv_b, dt_bias, a_log, d_skip, ssd_norm, w_out):
    b, L, _ = xn.shape
    proj = xn @ w_in
    q, k, v, z, xbc, dt = jnp.split(proj, HYB_SPLITS, axis=-1)
    q = q.reshape(b, L, SB_HEADS, SB_HEAD_DIM)
    k = k.reshape(b, L, SB_HEADS, SB_HEAD_DIM)
    v = v.reshape(b, L, SB_HEADS, SB_HEAD_DIM)
    past = k_cache.shape[1]
    k_all = jnp.concatenate([k_cache, k], axis=1)
    v_all = jnp.concatenate([v_cache, v], axis=1)
    o_sb = sb_attention(q, k_all, v_all, past).reshape(b, L, SB_WIDTH)
    xbc_c, new_buf = causal_conv(xbc, conv_buf, conv_w, conv_b)
    xbc_c = jax.nn.silu(xbc_c)
    xs, bm, cm = jnp.split(xbc_c, (SSD_INNER, SSD_INNER + SSD_GROUPS * SSD_STATE), axis=-1)
    xs = xs.reshape(b, L, SSD_HEADS, SSD_HEAD_DIM)
    bm = bm.reshape(b, L, SSD_GROUPS, SSD_STATE)
    cm = cm.reshape(b, L, SSD_GROUPS, SSD_STATE)
    dtp = jax.nn.softplus(dt.astype(F32) + dt_bias.astype(F32))
    a = -jnp.exp(a_log.astype(F32))
    y, h_new = ssd_scan(xs, dtp, a, bm, cm, ssd_h.astype(F32))
    y = y + d_skip.astype(F32)[:, None] * xs.astype(F32)
    y = y.reshape(b, L, SSD_INNER) * jax.nn.silu(z.astype(F32))
    yg = y.reshape(b, L, SSD_GROUPS, SSD_INNER // SSD_GROUPS)
    yg = yg * lax.rsqrt(jnp.mean(yg * yg, axis=-1, keepdims=True) + EPS)
    y = yg.reshape(b, L, SSD_INNER) * ssd_norm.astype(F32)
    out = jnp.concatenate([o_sb, y.astype(xn.dtype)], axis=-1) @ w_out
    return out, k, v, new_buf, h_new.astype(ssd_h.dtype)


RW_PAIRS = RW_HEADS // 2
RW_NP = 4
RW_TC = 64


def _rwkv_scan_kernel(r_ref, d_ref, k_ref, v_ref, a_ref, b_ref, s0_ref, o_ref, st_ref, s_scr, vcol_scr, *, tc):
    c = pl.program_id(2)

    @pl.when(c == 0)
    def _():
        s_scr[...] = s0_ref[0]

    row = lax.broadcasted_iota(jnp.int32, (RW_HEAD, 128), 0)
    lane = lax.broadcasted_iota(jnp.int32, (RW_HEAD, 128), 1)
    left = lane < RW_HEAD
    diag = (lane & (RW_HEAD - 1)) == row
    same_head = ((lax.broadcasted_iota(jnp.int32, (128, 128), 0) < RW_HEAD)
                 == (lax.broadcasted_iota(jnp.int32, (128, 128), 1) < RW_HEAD)).astype(jnp.bfloat16)
    same_head3 = jnp.concatenate([same_head] * 3, axis=0)

    def half_sums(x):
        lo = jnp.sum(jnp.where(left, x, 0.0), axis=1, keepdims=True)
        hi = jnp.sum(jnp.where(left, 0.0, x), axis=1, keepdims=True)
        return jnp.where(left, lo, hi)

    def vcol_body(t8, carry):
        base = pl.multiple_of(t8 * 8, 8)
        for p in range(RW_NP):
            terms = _terms3(v_ref[0, pl.ds(base, 8), p * 128:(p + 1) * 128])
            for i in range(8):
                h, m, l = (jnp.where(diag, t[i:i + 1], 0.0).astype(jnp.bfloat16) for t in terms)
                vcol_scr[p, base + i] = jnp.dot(jnp.concatenate([h, m, l], axis=1), same_head3,
                                                preferred_element_type=F32)
        return carry

    lax.fori_loop(0, tc // 8, vcol_body, 0)

    def step8(t8, carry):
        base = pl.multiple_of(t8 * 8, 8)
        lanes = [slice(p * 128, (p + 1) * 128) for p in range(RW_NP)]
        blk = [[ref[0, pl.ds(base, 8), sl] for ref in (a_ref, d_ref, b_ref, k_ref, r_ref)] for sl in lanes]
        state = [s_scr[p] for p in range(RW_NP)]
        rows = [[] for _ in range(RW_NP)]
        for i in range(8):
            for p in range(RW_NP):
                a8, d8, b8, k8, r8 = blk[p]
                s = state[p]
                sa = half_sums(s * a8[i:i + 1])
                s = s * d8[i:i + 1] + sa * b8[i:i + 1] + vcol_scr[p, base + i] * k8[i:i + 1]
                state[p] = s
                h, m, l = _split3(s * r8[i:i + 1])
                oc = jnp.dot(jnp.concatenate([h, m, l], axis=1), same_head3, preferred_element_type=F32)
                rows[p].append(jnp.sum(jnp.where(diag, oc, 0.0), axis=0, keepdims=True))
        for p in range(RW_NP):
            s_scr[p] = state[p]
            o_ref[0, pl.ds(base, 8), lanes[p]] = jnp.concatenate(rows[p], axis=0)
        return carry

    lax.fori_loop(0, tc // 8, step8, 0)

    @pl.when(c == pl.num_programs(2) - 1)
    def _():
        st_ref[0] = s_scr[...]


def rwkv_scan(r, d, k, v, a, b, wkv):
    bsz, L, D = r.shape
    tc = min(RW_TC, L)
    s0 = wkv.reshape(bsz, RW_PAIRS, 2, RW_HEAD, RW_HEAD).transpose(0, 1, 3, 2, 4).reshape(bsz, RW_PAIRS, RW_HEAD, 128)
    seq = pl.BlockSpec((1, tc, RW_NP * 128), lambda bi, g, c: (bi, c, g))
    st = pl.BlockSpec((1, RW_NP, RW_HEAD, 128), lambda bi, g, c: (bi, g, 0, 0))
    o, s_new = pl.pallas_call(
        functools.partial(_rwkv_scan_kernel, tc=tc),
        grid=(bsz, RW_PAIRS // RW_NP, L // tc),
        in_specs=[seq] * 6 + [st],
        out_specs=[seq, st],
        out_shape=[jax.ShapeDtypeStruct((bsz, L, D), F32),
                   jax.ShapeDtypeStruct((bsz, RW_PAIRS, RW_HEAD, 128), F32)],
        scratch_shapes=[pltpu.VMEM((RW_NP, RW_HEAD, 128), F32),
                        pltpu.VMEM((RW_NP, tc, RW_HEAD, 128), F32)],
        compiler_params=_cparams("parallel", "parallel", "arbitrary"),
        name="rwkv_scan",
    )(r, d, k, v, a, b, s0)
    s_new = s_new.reshape(bsz, RW_PAIRS, RW_HEAD, 2, RW_HEAD).transpose(0, 1, 3, 2, 4).reshape(bsz, RW_HEADS, RW_HEAD, RW_HEAD)
    return o, s_new


def _proj3_kernel(x_ref, w_ref, o_ref):
    o_ref[0] = jnp.dot(x_ref[0], w_ref[0], preferred_element_type=F32)


def proj3(x3, w3):
    _, m, kdim = x3.shape
    n = w3.shape[2]
    tm = min(m, 512)
    tn = min(n, 512)
    return pl.pallas_call(
        _proj3_kernel,
        grid=(3, m // tm, n // tn),
        in_specs=[pl.BlockSpec((1, tm, kdim), lambda i, a, b: (i, a, 0)),
                  pl.BlockSpec((1, kdim, tn), lambda i, a, b: (i, 0, b))],
        out_specs=pl.BlockSpec((1, tm, tn), lambda i, a, b: (i, a, b)),
        out_shape=jax.ShapeDtypeStruct((3, m, n), F32),
        compiler_params=_cparams("parallel", "parallel", "parallel"),
        name="rwkv_proj3",
    )(x3, w3)


def rwkv7_mixer(xn, shift_buf, wkv, mu, w_rkv, w0, w1, w2, a0, a1, a2, g1, g2,
                k_k, k_a, r_k, lnx_w, lnx_b, w_out):
    b, L, D = xn.shape
    x_prev = jnp.concatenate([shift_buf, xn[:, :-1]], axis=1)
    xx = x_prev - xn
    xm = xn[None] + xx[None] * mu[:, None, None, :]
    r, k, v = proj3(xm[:3].astype(jnp.bfloat16).reshape(3, b * L, D),
                    w_rkv.astype(jnp.bfloat16)).reshape(3, b, L, D)
    xw, xa, xg = xm[3], xm[4], xm[5]
    w = -jax.nn.softplus(-(w0 + jnp.tanh(xw @ w1) @ w2).astype(F32)) - 0.5
    decay = jnp.exp(-jnp.exp(w))
    a = jax.nn.sigmoid((a0 + (xa @ a1) @ a2).astype(F32))
    g = (jax.nn.sigmoid(xg @ g1) @ g2).astype(F32)
    hs = (b, L, RW_HEADS, RW_HEAD)
    kk = (k * k_k.astype(F32)).reshape(hs)
    kk = kk / jnp.maximum(jnp.sqrt(jnp.sum(kk * kk, axis=-1, keepdims=True)), 1e-12)
    k = k * (1.0 + (a - 1.0) * k_a.astype(F32))
    rh, dh, kh, vh, ah = (t.reshape(hs) for t in (r, decay, k, v, a))
    a_vec = -kk
    b_vec = kk * ah
    o, S_last = rwkv_scan(r, decay, k, v, a_vec.reshape(b, L, D), b_vec.reshape(b, L, D), wkv.astype(F32))
    o = o.reshape(hs)
    mean = jnp.mean(o, axis=-1, keepdims=True)
    var = jnp.mean(jnp.square(o - mean), axis=-1, keepdims=True)
    o = ((o - mean) * lax.rsqrt(var + RW_LNX_EPS)).reshape(b, L, D)
    o = o * lnx_w.astype(F32) + lnx_b.astype(F32)
    bonus = jnp.sum(rh * kh * r_k.astype(F32), axis=-1, keepdims=True) * vh
    o = o + bonus.reshape(b, L, D)
    out = (o * g).astype(xn.dtype) @ w_out
    return out, xn[:, -1:], S_last.astype(wkv.dtype)


PEER_TM = 256
PEER_TB = 32
PEER_NSTAGE = 8
PEER_HALF = D_MODEL // 2
PEER_SUB = PEER_HALF // 128
PEER_SEL = PEER_HEADS * PEER_TOPK


def _peer_scores_kernel(x_ref, g_ref, wq_ref, keys_ref, s_ref, xn_ref, xnb_ref):
    @pl.when(pl.program_id(1) == 0)
    def _():
        x = x_ref[...]
        xn = x * lax.rsqrt(jnp.mean(x * x, axis=-1, keepdims=True) + EPS) * g_ref[...]
        xn_ref[...] = xn
        xnb_ref[...] = xn.astype(jnp.bfloat16)

    q = jnp.dot(xnb_ref[...], wq_ref[...], preferred_element_type=F32).astype(jnp.bfloat16)
    half = PEER_QDIM // 2
    for i in range(2):
        s_ref[i * PEER_NKEYS:(i + 1) * PEER_NKEYS, :] = lax.dot_general(
            keys_ref[0, i], q[:, i * half:(i + 1) * half],
            (((1,), (1,)), ((), ())), preferred_element_type=F32)


def peer_scores(x, g, wq_bf, keys_bf):
    t, d = x.shape
    return pl.pallas_call(
        _peer_scores_kernel,
        grid=(t // PEER_TM, PEER_HEADS),
        in_specs=[pl.BlockSpec((PEER_TM, d), lambda i, h: (i, 0)),
                  pl.BlockSpec((1, d), lambda i, h: (0, 0)),
                  pl.BlockSpec((d, PEER_QDIM), lambda i, h: (0, h)),
                  pl.BlockSpec((1, 2, PEER_NKEYS, PEER_QDIM // 2), lambda i, h: (h, 0, 0, 0))],
        out_specs=[pl.BlockSpec((2 * PEER_NKEYS, PEER_TM), lambda i, h: (h, i)),
                   pl.BlockSpec((PEER_TM, d), lambda i, h: (i, 0))],
        out_shape=[jax.ShapeDtypeStruct((PEER_HEADS * 2 * PEER_NKEYS, t), F32),
                   jax.ShapeDtypeStruct((t, d), F32)],
        scratch_shapes=[pltpu.VMEM((PEER_TM, d), jnp.bfloat16)],
        compiler_params=_cparams("parallel", "arbitrary"),
        name="peer_scores",
    )(x, g.reshape(1, d), wq_bf, keys_bf)


PEER_TT = 128
NEG_INF = float("-inf")


def _peer_candidates():
    k = PEER_TOPK
    a_idx, b_idx, valid = [], [], []
    for a in range(k // 2):
        nb = k // (a + 1)
        rows = k if a == 0 else 8
        for b in range(rows):
            a_idx.append(a); b_idx.append(b); valid.append(b < nb)
    for a in range(k // 2, k):
        a_idx.append(a); b_idx.append(0); valid.append(True)
    return np.array(a_idx), np.array(b_idx), np.array(valid)


_CAND_A, _CAND_B, _CAND_VALID = _peer_candidates()
PEER_NCAND = len(_CAND_A)


def _top_rows(xs, tag, k):
    big = jnp.float32(1e9)
    xs = list(xs)
    vals = [[] for _ in xs]
    tags = [[] for _ in xs]
    for _ in range(k):
        for n, x in enumerate(xs):
            m = jnp.max(x, axis=0, keepdims=True)
            t = jnp.min(jnp.where(x == m, tag, big), axis=0, keepdims=True)
            vals[n].append(m)
            tags[n].append(t)
            xs[n] = jnp.where(tag == t, NEG_INF, x)
    return vals, tags


def _peer_topk_kernel(s_ref, pos_ref, eid_ref, gate_ref, v_scr, i_scr, e_scr, g_scr):
    k = PEER_TOPK
    key_tag = lax.broadcasted_iota(jnp.int32, (PEER_NKEYS, PEER_TT), 0).astype(F32)

    def half_body(h, carry):
        starts = [pl.multiple_of((2 * h + i) * PEER_NKEYS, PEER_NKEYS) for i in range(2)]
        vals, tags = _top_rows([s_ref[pl.ds(st, PEER_NKEYS), :] for st in starts], key_tag, k)
        for i in range(2):
            v_scr[2 * h + i] = jnp.concatenate(vals[i], axis=0)
            i_scr[2 * h + i] = jnp.concatenate(tags[i], axis=0)
        return carry

    lax.fori_loop(0, PEER_HEADS, half_body, 0)

    pos = pos_ref[...]
    valid = pos >= 0.0

    def candidates(h):
        s1 = v_scr[2 * h]; s2 = v_scr[2 * h + 1]
        i1 = i_scr[2 * h]; i2 = i_scr[2 * h + 1]
        e2 = i2[0:8]
        pieces_s = [s1[0:1] + s2]
        pieces_e = [i1[0:1] * PEER_NKEYS + i2]
        for a in range(1, k // 2):
            pieces_s.append(s1[a:a + 1] + s2[0:8])
            pieces_e.append(i1[a:a + 1] * PEER_NKEYS + e2)
        pieces_s.append(s1[k // 2:k] + s2[0:1])
        pieces_e.append(i1[k // 2:k] * PEER_NKEYS + i2[0:1])
        return jnp.where(valid, jnp.concatenate(pieces_s, axis=0), NEG_INF), jnp.concatenate(pieces_e, axis=0)

    def head_pair_body(hp, carry):
        heads = [2 * hp, 2 * hp + 1]
        cands, cids = zip(*[candidates(h) for h in heads])
        tops, tags = _top_rows(cands, pos, k)
        for n, h in enumerate(heads):
            eids = [jnp.max(jnp.where(pos == t, cids[n], -1.0), axis=0, keepdims=True) for t in tags[n]]
            top = jnp.concatenate(tops[n], axis=0)
            p = jnp.exp(top - top[0:1])
            gate = p / jnp.sum(p, axis=0, keepdims=True)
            row = pl.multiple_of(h * k, k)
            e_scr[pl.ds(row, k), :] = jnp.concatenate(eids, axis=0)
            g_scr[pl.ds(row, k), :] = gate
        return carry

    lax.fori_loop(0, PEER_HEADS // 2, head_pair_body, 0)
    eid_ref[...] = e_scr[...].T
    gate_ref[...] = g_scr[...].T


def peer_topk(scores_t):
    t = scores_t.shape[1]
    pos = np.where(_CAND_VALID, _CAND_A * PEER_TOPK + _CAND_B, -1).astype(np.float32)
    pos = jnp.asarray(np.broadcast_to(pos[:, None], (PEER_NCAND, PEER_TT)))
    return pl.pallas_call(
        _peer_topk_kernel,
        grid=(t // PEER_TT,),
        in_specs=[pl.BlockSpec((2 * PEER_HEADS * PEER_NKEYS, PEER_TT), lambda i: (0, i)),
                  pl.BlockSpec((PEER_NCAND, PEER_TT), lambda i: (0, 0))],
        out_specs=[pl.BlockSpec((PEER_TT, PEER_SEL), lambda i: (i, 0)),
                   pl.BlockSpec((PEER_TT, PEER_SEL), lambda i: (i, 0))],
        out_shape=[jax.ShapeDtypeStruct((t, PEER_SEL), F32),
                   jax.ShapeDtypeStruct((t, PEER_SEL), F32)],
        scratch_shapes=[pltpu.VMEM((2 * PEER_HEADS, PEER_TOPK, PEER_TT), F32),
                        pltpu.VMEM((2 * PEER_HEADS, PEER_TOPK, PEER_TT), F32),
                        pltpu.VMEM((PEER_SEL, PEER_TT), F32),
                        pltpu.VMEM((PEER_SEL, PEER_TT), F32)],
        compiler_params=_cparams("parallel"),
        name="peer_topk",
    )(scores_t, pos)


def pack_table(tab):
    e = tab.shape[0]
    b = lax.bitcast_convert_type(tab.astype(jnp.bfloat16), jnp.uint16).astype(jnp.uint32)
    b = b.reshape(e, 2, PEER_SUB, 128)
    return (b[:, 0] << 16) | b[:, 1]


def _terms3(x):
    h = x.astype(jnp.bfloat16).astype(F32)
    r = x - h
    m = r.astype(jnp.bfloat16).astype(F32)
    l = (r - m).astype(jnp.bfloat16).astype(F32)
    return h, m, l


def _stack_terms(terms, s):
    row = lax.broadcasted_iota(jnp.int32, (8, 128), 0)
    h, m, l = (t[s:s + 1] for t in terms)
    return jnp.where(row == 0, h, jnp.where(row == 1, m, jnp.where(row == 2, l, 0.0))).astype(jnp.bfloat16)


def _stage_rows(eid_ref, tab_ref, w_scr, t):
    for j in range(PEER_SEL):
        w_scr[PEER_SUB * j:PEER_SUB * (j + 1), :] = tab_ref[eid_ref[t, j]]


def _token_groups(eid_ref, tab_ref, w_scrs, compute):
    n = len(w_scrs)

    def group(i, carry):
        for k in range(n):
            _stage_rows(eid_ref, tab_ref, w_scrs[k], n * i + k)
        for k in range(n):
            compute(n * i + k, w_scrs[k])
        return carry

    lax.fori_loop(0, PEER_TB // n, group, 0)


def _halves(w_scr, s):
    w = w_scr[pl.ds(s, PEER_SEL, stride=PEER_SUB), :]
    hi = lax.bitcast_convert_type(w & jnp.uint32(0xFFFF0000), F32).astype(jnp.bfloat16)
    lo = lax.bitcast_convert_type(w << 16, F32).astype(jnp.bfloat16)
    return hi, lo


def _peer_hidden_kernel(eid_ref, x_ref, tab_ref, hid_ref, *w_scrs):
    nt = (((1,), (1,)), ((), ()))

    def compute(t, w_scr):
        th = _terms3(x_ref[t, 0])
        tl = _terms3(x_ref[t, 1])
        acc = jnp.zeros((8, PEER_SEL), F32)
        for s in range(PEER_SUB):
            hi, lo = _halves(w_scr, s)
            acc = acc + lax.dot_general(_stack_terms(th, s), hi, nt, preferred_element_type=F32)
            acc = acc + lax.dot_general(_stack_terms(tl, s), lo, nt, preferred_element_type=F32)
        hid_ref[t] = acc[0:1] + acc[1:2] + acc[2:3]

    _token_groups(eid_ref, tab_ref, w_scrs, compute)


def _peer_out_kernel(eid_ref, act_ref, res_ref, tab_ref, out_ref, *w_scrs):
    def compute(t, w_scr):
        a8 = _stack_terms(_terms3(act_ref[t]), 0)
        for s in range(PEER_SUB):
            hi, lo = _halves(w_scr, s)
            yh = jnp.dot(a8, hi, preferred_element_type=F32)
            yl = jnp.dot(a8, lo, preferred_element_type=F32)
            out_ref[t, 0, s:s + 1, :] = res_ref[t, 0, s:s + 1, :] + (yh[0:1] + yh[1:2] + yh[2:3])
            out_ref[t, 1, s:s + 1, :] = res_ref[t, 1, s:s + 1, :] + (yl[0:1] + yl[1:2] + yl[2:3])

    _token_groups(eid_ref, tab_ref, w_scrs, compute)


def _table_spec(n_exp):
    return pl.BlockSpec((n_exp, PEER_SUB, 128), lambda i: (0, 0, 0), pipeline_mode=pl.Buffered(1))


def peer_hidden(eid, xn_split, tab_packed):
    t = eid.shape[0]
    n_exp = tab_packed.shape[0]
    return pl.pallas_call(
        _peer_hidden_kernel,
        grid=(t // PEER_TB,),
        in_specs=[pl.BlockSpec((PEER_TB, PEER_SEL), lambda i: (i, 0), memory_space=pltpu.SMEM),
                  pl.BlockSpec((PEER_TB, 2, PEER_SUB, 128), lambda i: (i, 0, 0, 0)),
                  _table_spec(n_exp)],
        out_specs=pl.BlockSpec((PEER_TB, 1, PEER_SEL), lambda i: (i, 0, 0)),
        out_shape=jax.ShapeDtypeStruct((t, 1, PEER_SEL), F32),
        scratch_shapes=[pltpu.VMEM((PEER_SEL * PEER_SUB, 128), jnp.uint32)] * PEER_NSTAGE,
        compiler_params=_cparams("parallel"),
        name="peer_hidden",
    )(eid, xn_split, tab_packed)


def peer_out(eid, act, res_split, tab_packed):
    t = eid.shape[0]
    n_exp = tab_packed.shape[0]
    return pl.pallas_call(
        _peer_out_kernel,
        grid=(t // PEER_TB,),
        in_specs=[pl.BlockSpec((PEER_TB, PEER_SEL), lambda i: (i, 0), memory_space=pltpu.SMEM),
                  pl.BlockSpec((PEER_TB, 1, PEER_SEL), lambda i: (i, 0, 0)),
                  pl.BlockSpec((PEER_TB, 2, PEER_SUB, 128), lambda i: (i, 0, 0, 0)),
                  _table_spec(n_exp)],
        out_specs=pl.BlockSpec((PEER_TB, 2, PEER_SUB, 128), lambda i: (i, 0, 0, 0)),
        out_shape=jax.ShapeDtypeStruct((t, 2, PEER_SUB, 128), F32),
        scratch_shapes=[pltpu.VMEM((PEER_SEL * PEER_SUB, 128), jnp.uint32)] * PEER_NSTAGE,
        compiler_params=_cparams("parallel"),
        name="peer_out",
    )(eid, act.reshape(t, 1, PEER_SEL), res_split, tab_packed)


def peer_layer(x, ng, w_q, keys, u_packed, v_packed):
    t, d = x.shape
    scores, xn = peer_scores(x, ng, w_q.astype(jnp.bfloat16), keys.astype(jnp.bfloat16))
    eid_f, gate = peer_topk(scores)
    eid = eid_f.astype(jnp.int32)
    hidden = peer_hidden(eid, xn.reshape(t, 2, PEER_SUB, 128), u_packed).reshape(t, PEER_SEL)
    act = jax.nn.gelu(hidden, approximate=False) * gate
    out = peer_out(eid, act, x.reshape(t, 2, PEER_SUB, 128), v_packed)
    return out.reshape(t, d)


def mixer_layer(l, x, states, mp):
    xn = rmsnorm(x, mp[0])
    if l % 2 == 0:
        out, *ns = hybrid_mixer(xn, *states, *mp[1:])
    else:
        out, *ns = rwkv7_mixer(xn, *states, *mp[1:])
    return x + out, ns


def run_trunks(xs, states_list, layer_params, peer_params, final_norm):
    new_states = [[] for _ in xs]
    sizes = [x.shape[0] * x.shape[1] for x in xs]
    offs = np.cumsum([0] + sizes)
    for l in range(DEPTH):
        mixed = []
        for gi, x in enumerate(xs):
            x, ns = mixer_layer(l, x, states_list[gi][l], layer_params[l])
            new_states[gi].extend(ns)
            mixed.append(x.reshape(-1, D_MODEL))
        ng, wq, keys, u_packed, v_packed = peer_params[l]
        y = peer_layer(jnp.concatenate(mixed, axis=0), ng, wq, keys, u_packed, v_packed)
        xs = [y[offs[gi]:offs[gi + 1]].reshape(xs[gi].shape) for gi in range(len(xs))]
    ys = [pallas_rmsnorm(x, final_norm) for x in xs]
    return ys, new_states


def kernel(x_prompt, x_sample, cache_sb_k, cache_sb_v, state_ssd_conv, state_ssd,
           state_rwkv_shift, state_rwkv_wkv,
           l0_norm_mix, l0_w_in, l0_conv_w, l0_conv_b, l0_dt_bias, l0_a_log, l0_d_skip,
           l0_ssd_norm, l0_w_out,
           l1_norm_mix, l1_mu, l1_w_rkv, l1_w0, l1_w1, l1_w2, l1_a0, l1_a1, l1_a2,
           l1_g1, l1_g2, l1_k_k, l1_k_a, l1_r_k, l1_lnx_w, l1_lnx_b, l1_w_out,
           l0_norm_ffn, l0_peer_wq, l0_peer_keys, l0_peer_u, l0_peer_v,
           l1_norm_ffn, l1_peer_wq, l1_peer_keys, l1_peer_u, l1_peer_v,
           final_norm):
    layer_params = [
        (l0_norm_mix, l0_w_in, l0_conv_w, l0_conv_b, l0_dt_bias, l0_a_log, l0_d_skip,
         l0_ssd_norm, l0_w_out),
        (l1_norm_mix, l1_mu, l1_w_rkv, l1_w0, l1_w1, l1_w2, l1_a0, l1_a1, l1_a2,
         l1_g1, l1_g2, l1_k_k, l1_k_a, l1_r_k, l1_lnx_w, l1_lnx_b, l1_w_out),
    ]
    peer_params = [
        (l0_norm_ffn, l0_peer_wq, l0_peer_keys, pack_table(l0_peer_u), pack_table(l0_peer_v)),
        (l1_norm_ffn, l1_peer_wq, l1_peer_keys, pack_table(l1_peer_u), pack_table(l1_peer_v)),
    ]
    bp = x_prompt.shape[0]
    dt = x_prompt.dtype
    zero_states = [
        (jnp.zeros((bp, 0, SB_HEADS, SB_HEAD_DIM), dt), jnp.zeros((bp, 0, SB_HEADS, SB_HEAD_DIM), dt),
         jnp.zeros((bp, SSD_CONV - 1, SSD_CONV_DIM), dt),
         jnp.zeros((bp, SSD_HEADS, SSD_HEAD_DIM, SSD_STATE), dt)),
        (jnp.zeros((bp, 1, D_MODEL), dt), jnp.zeros((bp, RW_HEADS, RW_HEAD, RW_HEAD), dt)),
    ]
    sample_states = [
        (cache_sb_k, cache_sb_v, state_ssd_conv, state_ssd),
        (state_rwkv_shift, state_rwkv_wkv),
    ]
    (y_prompt, y_sample), (p_states, s_states) = run_trunks(
        [x_prompt, x_sample], [zero_states, sample_states], layer_params, peer_params, final_norm)
    p_sb_k, p_sb_v, p_conv, p_ssd, p_shift, p_wkv = p_states
    s_sb_k, s_sb_v, s_conv, s_ssd, s_shift, s_wkv = s_states
    return (y_prompt, y_sample, p_sb_k, p_sb_v, p_conv, p_ssd, p_shift, p_wkv,
            s_sb_k, s_sb_v, s_conv, s_ssd, s_shift, s_wkv)
```

```python
import functools
import math
import jax, jax.numpy as jnp
from jax import lax
import numpy as np
from jax.experimental import pallas as pl
from jax.experimental.pallas import tpu as pltpu

D_MODEL = 1024
DEPTH = 2

CHUNK = 64
EPS = 1e-6
SB_HEADS = 8
SB_HEAD_DIM = 64
SB_WIDTH = SB_HEADS * SB_HEAD_DIM
SB_BLOCK = 128
SSD_HEADS = 8
SSD_HEAD_DIM = 64
SSD_INNER = SSD_HEADS * SSD_HEAD_DIM
SSD_GROUPS = 2
SSD_STATE = 64
SSD_CONV = 4
SSD_CONV_DIM = SSD_INNER + 2 * SSD_GROUPS * SSD_STATE
SSD_CHUNK = CHUNK
HYB_SPLITS = (SB_WIDTH, 2 * SB_WIDTH, 3 * SB_WIDTH, 3 * SB_WIDTH + SSD_INNER,
              3 * SB_WIDTH + SSD_INNER + SSD_CONV_DIM)
HYB_IN = 3 * SB_WIDTH + SSD_INNER + SSD_CONV_DIM + SSD_HEADS
RW_HEAD = 64
RW_HEADS = D_MODEL // RW_HEAD
RW_LNX_EPS = 64e-5
PEER_HEADS = 8
PEER_NKEYS = 128
PEER_QDIM = 256
PEER_TOPK = 16
PEER_BLOCK = 128

F32 = jnp.float32
V7X_VMEM_BYTES = 64 * 1024 * 1024


def _cparams(*semantics):
    return pltpu.CompilerParams(dimension_semantics=semantics, vmem_limit_bytes=V7X_VMEM_BYTES)


def rmsnorm(x, g):
    xf = x.astype(F32)
    y = xf * lax.rsqrt(jnp.mean(xf * xf, axis=-1, keepdims=True) + EPS)
    return (y * g.astype(F32)).astype(x.dtype)


def _rmsnorm_kernel(x_ref, g_ref, o_ref):
    x = x_ref[...]
    y = x * lax.rsqrt(jnp.mean(x * x, axis=-1, keepdims=True) + EPS)
    o_ref[...] = y * g_ref[...]


def pallas_rmsnorm(x, g, rows=256):
    shp = x.shape
    x2 = x.reshape(-1, shp[-1])
    t, d = x2.shape
    rows = min(rows, t)
    out = pl.pallas_call(
        _rmsnorm_kernel,
        grid=(t // rows,),
        in_specs=[pl.BlockSpec((rows, d), lambda i: (i, 0)),
                  pl.BlockSpec((1, d), lambda i: (0, 0))],
        out_specs=pl.BlockSpec((rows, d), lambda i: (i, 0)),
        out_shape=jax.ShapeDtypeStruct((t, d), x.dtype),
        compiler_params=_cparams("parallel"),
        name="final_rmsnorm",
    )(x2, g.reshape(1, d))
    return out.reshape(shp)


SB_KBLOCK = 128
SB_LOG_FLOOR = -90.0


def _split3(x):
    h = x.astype(jnp.bfloat16)
    r = x - h.astype(F32)
    m = r.astype(jnp.bfloat16)
    l = (r - m.astype(F32)).astype(jnp.bfloat16)
    return h, m, l


def _sb_attn_kernel(q_ref, k_ref, v_ref, o_ref, *, tq, past):
    iq = pl.program_id(2)
    q = q_ref[0, 0]
    q_pos = past + iq * tq + lax.broadcasted_iota(jnp.int32, (tq, SB_KBLOCK), 0)
    k_off = lax.broadcasted_iota(jnp.int32, (tq, SB_KBLOCK), 1)
    r = lax.broadcasted_iota(jnp.int32, (SB_KBLOCK, SB_KBLOCK), 0)
    c = lax.broadcasted_iota(jnp.int32, (SB_KBLOCK, SB_KBLOCK), 1)
    upper = (r > c).astype(jnp.bfloat16)
    j0 = (past + (iq + 1) * tq - 1) // SB_KBLOCK

    def cond(state):
        j, live, _, _ = state
        return jnp.logical_and(j >= 0, live > 0)

    def body(state):
        j, _, carry, acc = state
        start = pl.multiple_of(j * SB_KBLOCK, SB_KBLOCK)
        kb = k_ref[0, 0, pl.ds(start, SB_KBLOCK), :]
        vb = v_ref[0, 0, pl.ds(start, SB_KBLOCK), :]
        z = lax.dot_general(q, kb, (((1,), (1,)), ((), ())), preferred_element_type=F32)
        mask = (j * SB_KBLOCK + k_off) < q_pos
        sp = jnp.maximum(z, 0.0) + jnp.log1p(jnp.exp(-jnp.abs(z)))
        log_fail = jnp.where(mask, -sp, 0.0)
        h, m, l = _split3(log_fail)
        later = (jnp.dot(h, upper, preferred_element_type=F32)
                 + jnp.dot(m, upper, preferred_element_type=F32)
                 + jnp.dot(l, upper, preferred_element_type=F32))
        log_w = (z - sp) + later + carry
        w = jnp.where(mask, jnp.exp(log_w), 0.0)
        acc = acc + jnp.dot(w.astype(jnp.bfloat16), vb, preferred_element_type=F32)
        carry = carry + jnp.sum(log_fail, axis=-1, keepdims=True)
        live = (jnp.max(carry) > SB_LOG_FLOOR).astype(jnp.int32)
        return j - 1, live, carry, acc

    state = (j0, jnp.int32(1), jnp.zeros((tq, 1), F32), jnp.zeros((tq, SB_HEAD_DIM), F32))
    _, _, _, acc = lax.while_loop(cond, body, state)
    o_ref[0, 0] = acc


def sb_attention(q, k_all, v_all, past):
    b, lq, h, d = q.shape
    s = k_all.shape[1]
    tq = min(lq, 128)
    s_pad = -(-s // SB_KBLOCK) * SB_KBLOCK
    qh = (q * (d ** -0.5)).astype(jnp.bfloat16).transpose(0, 2, 1, 3)
    kh = jnp.pad(k_all.astype(jnp.bfloat16).transpose(0, 2, 1, 3), ((0, 0), (0, 0), (0, s_pad - s), (0, 0)))
    vh = jnp.pad(v_all.astype(jnp.bfloat16).transpose(0, 2, 1, 3), ((0, 0), (0, 0), (0, s_pad - s), (0, 0)))
    out = pl.pallas_call(
        functools.partial(_sb_attn_kernel, tq=tq, past=past),
        grid=(b, h, lq // tq),
        in_specs=[pl.BlockSpec((1, 1, tq, d), lambda bi, hi, qi: (bi, hi, qi, 0)),
                  pl.BlockSpec((1, 1, s_pad, d), lambda bi, hi, qi: (bi, hi, 0, 0)),
                  pl.BlockSpec((1, 1, s_pad, d), lambda bi, hi, qi: (bi, hi, 0, 0))],
        out_specs=pl.BlockSpec((1, 1, tq, d), lambda bi, hi, qi: (bi, hi, qi, 0)),
        out_shape=jax.ShapeDtypeStruct((b, h, lq, d), F32),
        compiler_params=_cparams("parallel", "parallel", "arbitrary"),
        name="sb_attention",
    )(qh, kh, vh)
    return out.transpose(0, 2, 1, 3)


def causal_conv(u, buf, w, bias):
    L = u.shape[1]
    full = jnp.concatenate([buf, u], axis=1)
    out = bias
    for i in range(SSD_CONV):
        out = out + full[:, i:i + L] * w[i]
    return out, full[:, -(SSD_CONV - 1):]


def ssd_scan(x, dt, a, bm, cm, h0):
    b, L, H, P = x.shape
    Q = SSD_CHUNK if L % SSD_CHUNK == 0 else L
    nc = L // Q
    rep = H // SSD_GROUPS
    xf = x.astype(F32).reshape(b, nc, Q, H, P)
    bh = jnp.repeat(bm.astype(F32), rep, axis=2).reshape(b, nc, Q, H, SSD_STATE)
    ch = jnp.repeat(cm.astype(F32), rep, axis=2).reshape(b, nc, Q, H, SSD_STATE)
    dtc = dt.reshape(b, nc, Q, H)
    a_cum = jnp.cumsum(dtc * a, axis=2)
    seg = a_cum[:, :, :, None, :] - a_cum[:, :, None, :, :]
    causal = jnp.tril(jnp.ones((Q, Q), dtype=bool))[None, None, :, :, None]
    decay = jnp.exp(jnp.where(causal, seg, -jnp.inf))
    cb = jnp.einsum('bclhn,bcshn->bclsh', ch, bh)
    y_diag = jnp.einsum('bclsh,bcshp->bclhp', cb * decay * dtc[:, :, None], xf)
    w_state = jnp.exp(a_cum[:, :, -1:] - a_cum) * dtc
    chunk_states = jnp.einsum('bclhn,bclh,bclhp->bchpn', bh, w_state, xf)
    chunk_decay = jnp.exp(a_cum[:, :, -1])

    def step(h, inp):
        dec, st = inp
        return dec[:, :, None, None] * h + st, h

    h_last, h_prev = lax.scan(step, h0, (chunk_decay.transpose(1, 0, 2),
                                          chunk_states.transpose(1, 0, 2, 3, 4)))
    h_prev = h_prev.transpose(1, 0, 2, 3, 4)
    y_off = jnp.einsum('bclhn,bchpn->bclhp', ch, h_prev) * jnp.exp(a_cum)[..., None]
    return (y_diag + y_off).reshape(b, L, H, P), h_last


def hybrid_mixer(xn, k_cache, v_cache, conv_buf, ssd_h,
                 w_in, conv_w, conv_b, dt_bias, a_log, d_skip, ssd_norm, w_out):
    b, L, _ = xn.shape
    proj = xn @ w_in
    q, k, v, z, xbc, dt = jnp.split(proj, HYB_SPLITS, axis=-1)
    q = q.reshape(b, L, SB_HEADS, SB_HEAD_DIM)
    k = k.reshape(b, L, SB_HEADS, SB_HEAD_DIM)
    v = v.reshape(b, L, SB_HEADS, SB_HEAD_DIM)
    past = k_cache.shape[1]
    k_all = jnp.concatenate([k_cache, k], axis=1)
    v_all = jnp.concatenate([v_cache, v], axis=1)
    o_sb = sb_attention(q, k_all, v_all, past).reshape(b, L, SB_WIDTH)
    xbc_c, new_buf = causal_conv(xbc, conv_buf, conv_w, conv_b)
    xbc_c = jax.nn.silu(xbc_c)
    xs, bm, cm = jnp.split(xbc_c, (SSD_INNER, SSD_INNER + SSD_GROUPS * SSD_STATE), axis=-1)
    xs = xs.reshape(b, L, SSD_HEADS, SSD_HEAD_DIM)
    bm = bm.reshape(b, L, SSD_GROUPS, SSD_STATE)
    cm = cm.reshape(b, L, SSD_GROUPS, SSD_STATE)
    dtp = jax.nn.softplus(dt.astype(F32) + dt_bias.astype(F32))
    a = -jnp.exp(a_log.astype(F32))
    y, h_new = ssd_scan(xs, dtp, a, bm, cm, ssd_h.astype(F32))
    y = y + d_skip.astype(F32)[:, None] * xs.astype(F32)
    y = y.reshape(b, L, SSD_INNER) * jax.nn.silu(z.astype(F32))
    yg = y.reshape(b, L, SSD_GROUPS, SSD_INNER // SSD_GROUPS)
    yg = yg * lax.rsqrt(jnp.mean(yg * yg, axis=-1, keepdims=True) + EPS)
    y = yg.reshape(b, L, SSD_INNER) * ssd_norm.astype(F32)
    out = jnp.concatenate([o_sb, y.astype(xn.dtype)], axis=-1) @ w_out
    return out, k, v, new_buf, h_new.astype(ssd_h.dtype)


RW_PAIRS = RW_HEADS // 2
RW_NP = 4
RW_TC = 64


def _rwkv_scan_kernel(r_ref, d_ref, k_ref, v_ref, a_ref, b_ref, s0_ref, o_ref, st_ref, s_scr, vcol_scr, *, tc):
    c = pl.program_id(2)

    @pl.when(c == 0)
    def _():
        s_scr[...] = s0_ref[0]

    row = lax.broadcasted_iota(jnp.int32, (RW_HEAD, 128), 0)
    lane = lax.broadcasted_iota(jnp.int32, (RW_HEAD, 128), 1)
    left = lane < RW_HEAD
    diag = (lane & (RW_HEAD - 1)) == row
    same_head = ((lax.broadcasted_iota(jnp.int32, (128, 128), 0) < RW_HEAD)
                 == (lax.broadcasted_iota(jnp.int32, (128, 128), 1) < RW_HEAD)).astype(jnp.bfloat16)
    same_head3 = jnp.concatenate([same_head] * 3, axis=0)

    def half_sums(x):
        lo = jnp.sum(jnp.where(left, x, 0.0), axis=1, keepdims=True)
        hi = jnp.sum(jnp.where(left, 0.0, x), axis=1, keepdims=True)
        return jnp.where(left, lo, hi)

    def vcol_body(t8, carry):
        base = pl.multiple_of(t8 * 8, 8)
        for p in range(RW_NP):
            terms = _terms3(v_ref[0, pl.ds(base, 8), p * 128:(p + 1) * 128])
            for i in range(8):
                h, m, l = (jnp.where(diag, t[i:i + 1], 0.0).astype(jnp.bfloat16) for t in terms)
                vcol_scr[p, base + i] = jnp.dot(jnp.concatenate([h, m, l], axis=1), same_head3,
                                                preferred_element_type=F32)
        return carry

    lax.fori_loop(0, tc // 8, vcol_body, 0)

    def step8(t8, carry):
        base = pl.multiple_of(t8 * 8, 8)
        lanes = [slice(p * 128, (p + 1) * 128) for p in range(RW_NP)]
        blk = [[ref[0, pl.ds(base, 8), sl] for ref in (a_ref, d_ref, b_ref, k_ref, r_ref)] for sl in lanes]
        state = [s_scr[p] for p in range(RW_NP)]
        rows = [[] for _ in range(RW_NP)]
        for i in range(8):
            for p in range(RW_NP):
                a8, d8, b8, k8, r8 = blk[p]
                s = state[p]
                sa = half_sums(s * a8[i:i + 1])
                s = s * d8[i:i + 1] + sa * b8[i:i + 1] + vcol_scr[p, base + i] * k8[i:i + 1]
                state[p] = s
                h, m, l = _split3(s * r8[i:i + 1])
                oc = jnp.dot(jnp.concatenate([h, m, l], axis=1), same_head3, preferred_element_type=F32)
                rows[p].append(jnp.sum(jnp.where(diag, oc, 0.0), axis=0, keepdims=True))
        for p in range(RW_NP):
            s_scr[p] = state[p]
            o_ref[0, pl.ds(base, 8), lanes[p]] = jnp.concatenate(rows[p], axis=0)
        return carry

    lax.fori_loop(0, tc // 8, step8, 0)

    @pl.when(c == pl.num_programs(2) - 1)
    def _():
        st_ref[0] = s_scr[...]


def rwkv_scan(r, d, k, v, a, b, wkv):
    bsz, L, D = r.shape
    tc = min(RW_TC, L)
    s0 = wkv.reshape(bsz, RW_PAIRS, 2, RW_HEAD, RW_HEAD).transpose(0, 1, 3, 2, 4).reshape(bsz, RW_PAIRS, RW_HEAD, 128)
    seq = pl.BlockSpec((1, tc, RW_NP * 128), lambda bi, g, c: (bi, c, g))
    st = pl.BlockSpec((1, RW_NP, RW_HEAD, 128), lambda bi, g, c: (bi, g, 0, 0))
    o, s_new = pl.pallas_call(
        functools.partial(_rwkv_scan_kernel, tc=tc),
        grid=(bsz, RW_PAIRS // RW_NP, L // tc),
        in_specs=[seq] * 6 + [st],
        out_specs=[seq, st],
        out_shape=[jax.ShapeDtypeStruct((bsz, L, D), F32),
                   jax.ShapeDtypeStruct((bsz, RW_PAIRS, RW_HEAD, 128), F32)],
        scratch_shapes=[pltpu.VMEM((RW_NP, RW_HEAD, 128), F32),
                        pltpu.VMEM((RW_NP, tc, RW_HEAD, 128), F32)],
        compiler_params=_cparams("parallel", "parallel", "arbitrary"),
        name="rwkv_scan",
    )(r, d, k, v, a, b, s0)
    s_new = s_new.reshape(bsz, RW_PAIRS, RW_HEAD, 2, RW_HEAD).transpose(0, 1, 3, 2, 4).reshape(bsz, RW_HEADS, RW_HEAD, RW_HEAD)
    return o, s_new


def _proj3_kernel(x_ref, w_ref, o_ref):
    o_ref[0] = jnp.dot(x_ref[0], w_ref[0], preferred_element_type=F32)


def proj3(x3, w3):
    _, m, kdim = x3.shape
    n = w3.shape[2]
    tm = min(m, 512)
    tn = min(n, 512)
    return pl.pallas_call(
        _proj3_kernel,
        grid=(3, m // tm, n // tn),
        in_specs=[pl.BlockSpec((1, tm, kdim), lambda i, a, b: (i, a, 0)),
                  pl.BlockSpec((1, kdim, tn), lambda i, a, b: (i, 0, b))],
        out_specs=pl.BlockSpec((1, tm, tn), lambda i, a, b: (i, a, b)),
        out_shape=jax.ShapeDtypeStruct((3, m, n), F32),
        compiler_params=_cparams("parallel", "parallel", "parallel"),
        name="rwkv_proj3",
    )(x3, w3)


def rwkv7_mixer(xn, shift_buf, wkv, mu, w_rkv, w0, w1, w2, a0, a1, a2, g1, g2,
                k_k, k_a, r_k, lnx_w, lnx_b, w_out):
    b, L, D = xn.shape
    x_prev = jnp.concatenate([shift_buf, xn[:, :-1]], axis=1)
    xx = x_prev - xn
    xm = xn[None] + xx[None] * mu[:, None, None, :]
    r, k, v = proj3(xm[:3].astype(jnp.bfloat16).reshape(3, b * L, D),
                    w_rkv.astype(jnp.bfloat16)).reshape(3, b, L, D)
    xw, xa, xg = xm[3], xm[4], xm[5]
    w = -jax.nn.softplus(-(w0 + jnp.tanh(xw @ w1) @ w2).astype(F32)) - 0.5
    decay = jnp.exp(-jnp.exp(w))
    a = jax.nn.sigmoid((a0 + (xa @ a1) @ a2).astype(F32))
    g = (jax.nn.sigmoid(xg @ g1) @ g2).astype(F32)
    hs = (b, L, RW_HEADS, RW_HEAD)
    kk = (k * k_k.astype(F32)).reshape(hs)
    kk = kk / jnp.maximum(jnp.sqrt(jnp.sum(kk * kk, axis=-1, keepdims=True)), 1e-12)
    k = k * (1.0 + (a - 1.0) * k_a.astype(F32))
    rh, dh, kh, vh, ah = (t.reshape(hs) for t in (r, decay, k, v, a))
    a_vec = -kk
    b_vec = kk * ah
    o, S_last = rwkv_scan(r, decay, k, v, a_vec.reshape(b, L, D), b_vec.reshape(b, L, D), wkv.astype(F32))
    o = o.reshape(hs)
    mean = jnp.mean(o, axis=-1, keepdims=True)
    var = jnp.mean(jnp.square(o - mean), axis=-1, keepdims=True)
    o = ((o - mean) * lax.rsqrt(var + RW_LNX_EPS)).reshape(b, L, D)
    o = o * lnx_w.astype(F32) + lnx_b.astype(F32)
    bonus = jnp.sum(rh * kh * r_k.astype(F32), axis=-1, keepdims=True) * vh
    o = o + bonus.reshape(b, L, D)
    out = (o * g).astype(xn.dtype) @ w_out
    return out, xn[:, -1:], S_last.astype(wkv.dtype)


PEER_TM = 256
PEER_TB = 64
PEER_NSTAGE = 16
PEER_HALF = D_MODEL // 2
PEER_SUB = PEER_HALF // 128
PEER_SEL = PEER_HEADS * PEER_TOPK


def _peer_scores_kernel(x_ref, g_ref, wq_ref, keys_ref, s_ref, xn_ref, xnb_ref):
    @pl.when(pl.program_id(1) == 0)
    def _():
        x = x_ref[...]
        xn = x * lax.rsqrt(jnp.mean(x * x, axis=-1, keepdims=True) + EPS) * g_ref[...]
        xn_ref[...] = xn
        xnb_ref[...] = xn.astype(jnp.bfloat16)

    q = jnp.dot(xnb_ref[...], wq_ref[...], preferred_element_type=F32).astype(jnp.bfloat16)
    half = PEER_QDIM // 2
    for i in range(2):
        s_ref[i * PEER_NKEYS:(i + 1) * PEER_NKEYS, :] = lax.dot_general(
            keys_ref[0, i], q[:, i * half:(i + 1) * half],
            (((1,), (1,)), ((), ())), preferred_element_type=F32)


def peer_scores(x, g, wq_bf, keys_bf):
    t, d = x.shape
    return pl.pallas_call(
        _peer_scores_kernel,
        grid=(t // PEER_TM, PEER_HEADS),
        in_specs=[pl.BlockSpec((PEER_TM, d), lambda i, h: (i, 0)),
                  pl.BlockSpec((1, d), lambda i, h: (0, 0)),
                  pl.BlockSpec((d, PEER_QDIM), lambda i, h: (0, h)),
                  pl.BlockSpec((1, 2, PEER_NKEYS, PEER_QDIM // 2), lambda i, h: (h, 0, 0, 0))],
        out_specs=[pl.BlockSpec((2 * PEER_NKEYS, PEER_TM), lambda i, h: (h, i)),
                   pl.BlockSpec((PEER_TM, d), lambda i, h: (i, 0))],
        out_shape=[jax.ShapeDtypeStruct((PEER_HEADS * 2 * PEER_NKEYS, t), F32),
                   jax.ShapeDtypeStruct((t, d), F32)],
        scratch_shapes=[pltpu.VMEM((PEER_TM, d), jnp.bfloat16)],
        compiler_params=_cparams("parallel", "arbitrary"),
        name="peer_scores",
    )(x, g.reshape(1, d), wq_bf, keys_bf)


PEER_TT = 128
NEG_INF = float("-inf")


def _peer_candidates():
    k = PEER_TOPK
    a_idx, b_idx, valid = [], [], []
    for a in range(k // 2):
        nb = k // (a + 1)
        rows = k if a == 0 else 8
        for b in range(rows):
            a_idx.append(a); b_idx.append(b); valid.append(b < nb)
    for a in range(k // 2, k):
        a_idx.append(a); b_idx.append(0); valid.append(True)
    return np.array(a_idx), np.array(b_idx), np.array(valid)


_CAND_A, _CAND_B, _CAND_VALID = _peer_candidates()
PEER_NCAND = len(_CAND_A)


def _top_rows(xs, tag, k):
    big = jnp.float32(1e9)
    xs = list(xs)
    vals = [[] for _ in xs]
    tags = [[] for _ in xs]
    for _ in range(k):
        for n, x in enumerate(xs):
            m = jnp.max(x, axis=0, keepdims=True)
            t = jnp.min(jnp.where(x == m, tag, big), axis=0, keepdims=True)
            vals[n].append(m)
            tags[n].append(t)
            xs[n] = jnp.where(tag == t, NEG_INF, x)
    return vals, tags


def _peer_topk_kernel(s_ref, pos_ref, eid_ref, gate_ref, v_scr, i_scr, e_scr, g_scr):
    k = PEER_TOPK
    key_tag = lax.broadcasted_iota(jnp.int32, (PEER_NKEYS, PEER_TT), 0).astype(F32)

    def half_body(h, carry):
        starts = [pl.multiple_of((2 * h + i) * PEER_NKEYS, PEER_NKEYS) for i in range(2)]
        vals, tags = _top_rows([s_ref[pl.ds(st, PEER_NKEYS), :] for st in starts], key_tag, k)
        for i in range(2):
            v_scr[2 * h + i] = jnp.concatenate(vals[i], axis=0)
            i_scr[2 * h + i] = jnp.concatenate(tags[i], axis=0)
        return carry

    lax.fori_loop(0, PEER_HEADS, half_body, 0)

    pos = pos_ref[...]
    valid = pos >= 0.0

    def candidates(h):
        s1 = v_scr[2 * h]; s2 = v_scr[2 * h + 1]
        i1 = i_scr[2 * h]; i2 = i_scr[2 * h + 1]
        e2 = i2[0:8]
        pieces_s = [s1[0:1] + s2]
        pieces_e = [i1[0:1] * PEER_NKEYS + i2]
        for a in range(1, k // 2):
            pieces_s.append(s1[a:a + 1] + s2[0:8])
            pieces_e.append(i1[a:a + 1] * PEER_NKEYS + e2)
        pieces_s.append(s1[k // 2:k] + s2[0:1])
        pieces_e.append(i1[k // 2:k] * PEER_NKEYS + i2[0:1])
        return jnp.where(valid, jnp.concatenate(pieces_s, axis=0), NEG_INF), jnp.concatenate(pieces_e, axis=0)

    def head_pair_body(hp, carry):
        heads = [2 * hp, 2 * hp + 1]
        cands, cids = zip(*[candidates(h) for h in heads])
        tops, tags = _top_rows(cands, pos, k)
        for n, h in enumerate(heads):
            eids = [jnp.max(jnp.where(pos == t, cids[n], -1.0), axis=0, keepdims=True) for t in tags[n]]
            top = jnp.concatenate(tops[n], axis=0)
            p = jnp.exp(top - top[0:1])
            gate = p / jnp.sum(p, axis=0, keepdims=True)
            row = pl.multiple_of(h * k, k)
            e_scr[pl.ds(row, k), :] = jnp.concatenate(eids, axis=0)
            g_scr[pl.ds(row, k), :] = gate
        return carry

    lax.fori_loop(0, PEER_HEADS // 2, head_pair_body, 0)
    eid_ref[...] = e_scr[...].T
    gate_ref[...] = g_scr[...].T


def peer_topk(scores_t):
    t = scores_t.shape[1]
    pos = np.where(_CAND_VALID, _CAND_A * PEER_TOPK + _CAND_B, -1).astype(np.float32)
    pos = jnp.asarray(np.broadcast_to(pos[:, None], (PEER_NCAND, PEER_TT)))
    return pl.pallas_call(
        _peer_topk_kernel,
        grid=(t // PEER_TT,),
        in_specs=[pl.BlockSpec((2 * PEER_HEADS * PEER_NKEYS, PEER_TT), lambda i: (0, i)),
                  pl.BlockSpec((PEER_NCAND, PEER_TT), lambda i: (0, 0))],
        out_specs=[pl.BlockSpec((PEER_TT, PEER_SEL), lambda i: (i, 0)),
                   pl.BlockSpec((PEER_TT, PEER_SEL), lambda i: (i, 0))],
        out_shape=[jax.ShapeDtypeStruct((t, PEER_SEL), F32),
                   jax.ShapeDtypeStruct((t, PEER_SEL), F32)],
        scratch_shapes=[pltpu.VMEM((2 * PEER_HEADS, PEER_TOPK, PEER_TT), F32),
                        pltpu.VMEM((2 * PEER_HEADS, PEER_TOPK, PEER_TT), F32),
                        pltpu.VMEM((PEER_SEL, PEER_TT), F32),
                        pltpu.VMEM((PEER_SEL, PEER_TT), F32)],
        compiler_params=_cparams("parallel"),
        name="peer_topk",
    )(scores_t, pos)


def pack_table(tab):
    e = tab.shape[0]
    b = lax.bitcast_convert_type(tab.astype(jnp.bfloat16), jnp.uint16).astype(jnp.uint32)
    b = b.reshape(e, 2, PEER_SUB, 128)
    return (b[:, 0] << 16) | b[:, 1]


def _terms3(x):
    h = x.astype(jnp.bfloat16).astype(F32)
    r = x - h
    m = r.astype(jnp.bfloat16).astype(F32)
    l = (r - m).astype(jnp.bfloat16).astype(F32)
    return h, m, l


def _stack_terms(terms, s):
    row = lax.broadcasted_iota(jnp.int32, (8, 128), 0)
    h, m, l = (t[s:s + 1] for t in terms)
    return jnp.where(row == 0, h, jnp.where(row == 1, m, jnp.where(row == 2, l, 0.0))).astype(jnp.bfloat16)


def _stage_rows(eid_ref, tab_ref, w_scr, t):
    for j in range(PEER_SEL):
        w_scr[PEER_SUB * j:PEER_SUB * (j + 1), :] = tab_ref[eid_ref[t, j]]


def _token_groups(eid_ref, tab_ref, w_scrs, compute):
    n = len(w_scrs)

    def group(i, carry):
        for k in range(n):
            _stage_rows(eid_ref, tab_ref, w_scrs[k], n * i + k)
        for k in range(n):
            compute(n * i + k, w_scrs[k])
        return carry

    lax.fori_loop(0, PEER_TB // n, group, 0)


def _halves(w_scr, s):
    w = w_scr[pl.ds(s, PEER_SEL, stride=PEER_SUB), :]
    hi = lax.bitcast_convert_type(w & jnp.uint32(0xFFFF0000), F32).astype(jnp.bfloat16)
    lo = lax.bitcast_convert_type(w << 16, F32).astype(jnp.bfloat16)
    return hi, lo


def _peer_hidden_kernel(eid_ref, x_ref, tab_ref, hid_ref, *w_scrs):
    nt = (((1,), (1,)), ((), ()))

    def compute(t, w_scr):
        th = _terms3(x_ref[t, 0])
        tl = _terms3(x_ref[t, 1])
        acc = jnp.zeros((8, PEER_SEL), F32)
        for s in range(PEER_SUB):
            hi, lo = _halves(w_scr, s)
            acc = acc + lax.dot_general(_stack_terms(th, s), hi, nt, preferred_element_type=F32)
            acc = acc + lax.dot_general(_stack_terms(tl, s), lo, nt, preferred_element_type=F32)
        hid_ref[t] = acc[0:1] + acc[1:2] + acc[2:3]

    _token_groups(eid_ref, tab_ref, w_scrs, compute)


def _peer_out_kernel(eid_ref, act_ref, res_ref, tab_ref, out_ref, *w_scrs):
    def compute(t, w_scr):
        a8 = _stack_terms(_terms3(act_ref[t]), 0)
        for s in range(PEER_SUB):
            hi, lo = _halves(w_scr, s)
            yh = jnp.dot(a8, hi, preferred_element_type=F32)
            yl = jnp.dot(a8, lo, preferred_element_type=F32)
            out_ref[t, 0, s:s + 1, :] = res_ref[t, 0, s:s + 1, :] + (yh[0:1] + yh[1:2] + yh[2:3])
            out_ref[t, 1, s:s + 1, :] = res_ref[t, 1, s:s + 1, :] + (yl[0:1] + yl[1:2] + yl[2:3])

    _token_groups(eid_ref, tab_ref, w_scrs, compute)


def _table_spec(n_exp):
    return pl.BlockSpec((n_exp, PEER_SUB, 128), lambda i: (0, 0, 0), pipeline_mode=pl.Buffered(1))


def peer_hidden(eid, xn_split, tab_packed):
    t = eid.shape[0]
    n_exp = tab_packed.shape[0]
    return pl.pallas_call(
        _peer_hidden_kernel,
        grid=(t // PEER_TB,),
        in_specs=[pl.BlockSpec((PEER_TB, PEER_SEL), lambda i: (i, 0), memory_space=pltpu.SMEM),
                  pl.BlockSpec((PEER_TB, 2, PEER_SUB, 128), lambda i: (i, 0, 0, 0)),
                  _table_spec(n_exp)],
        out_specs=pl.BlockSpec((PEER_TB, 1, PEER_SEL), lambda i: (i, 0, 0)),
        out_shape=jax.ShapeDtypeStruct((t, 1, PEER_SEL), F32),
        scratch_shapes=[pltpu.VMEM((PEER_SEL * PEER_SUB, 128), jnp.uint32)] * PEER_NSTAGE,
        compiler_params=_cparams("parallel"),
        name="peer_hidden",
    )(eid, xn_split, tab_packed)


def peer_out(eid, act, res_split, tab_packed):
    t = eid.shape[0]
    n_exp = tab_packed.shape[0]
    return pl.pallas_call(
        _peer_out_kernel,
        grid=(t // PEER_TB,),
        in_specs=[pl.BlockSpec((PEER_TB, PEER_SEL), lambda i: (i, 0), memory_space=pltpu.SMEM),
                  pl.BlockSpec((PEER_TB, 1, PEER_SEL), lambda i: (i, 0, 0)),
                  pl.BlockSpec((PEER_TB, 2, PEER_SUB, 128), lambda i: (i, 0, 0, 0)),
                  _table_spec(n_exp)],
        out_specs=pl.BlockSpec((PEER_TB, 2, PEER_SUB, 128), lambda i: (i, 0, 0, 0)),
        out_shape=jax.ShapeDtypeStruct((t, 2, PEER_SUB, 128), F32),
        scratch_shapes=[pltpu.VMEM((PEER_SEL * PEER_SUB, 128), jnp.uint32)] * PEER_NSTAGE,
        compiler_params=_cparams("parallel"),
        name="peer_out",
    )(eid, act.reshape(t, 1, PEER_SEL), res_split, tab_packed)


def peer_layer(x, ng, w_q, keys, u_packed, v_packed):
    t, d = x.shape
    scores, xn = peer_scores(x, ng, w_q.astype(jnp.bfloat16), keys.astype(jnp.bfloat16))
    eid_f, gate = peer_topk(scores)
    eid = eid_f.astype(jnp.int32)
    hidden = peer_hidden(eid, xn.reshape(t, 2, PEER_SUB, 128), u_packed).reshape(t, PEER_SEL)
    act = jax.nn.gelu(hidden, approximate=False) * gate
    out = peer_out(eid, act, x.reshape(t, 2, PEER_SUB, 128), v_packed)
    return out.reshape(t, d)


def mixer_layer(l, x, states, mp):
    xn = rmsnorm(x, mp[0])
    if l % 2 == 0:
        out, *ns = hybrid_mixer(xn, *states, *mp[1:])
    else:
        out, *ns = rwkv7_mixer(xn, *states, *mp[1:])
    return x + out, ns


def run_trunks(xs, states_list, layer_params, peer_params, final_norm):
    new_states = [[] for _ in xs]
    sizes = [x.shape[0] * x.shape[1] for x in xs]
    offs = np.cumsum([0] + sizes)
    for l in range(DEPTH):
        mixed = []
        for gi, x in enumerate(xs):
            x, ns = mixer_layer(l, x, states_list[gi][l], layer_params[l])
            new_states[gi].extend(ns)
            mixed.append(x.reshape(-1, D_MODEL))
        ng, wq, keys, u_packed, v_packed = peer_params[l]
        y = peer_layer(jnp.concatenate(mixed, axis=0), ng, wq, keys, u_packed, v_packed)
        xs = [y[offs[gi]:offs[gi + 1]].reshape(xs[gi].shape) for gi in range(len(xs))]
    ys = [pallas_rmsnorm(x, final_norm) for x in xs]
    return ys, new_states


def kernel(x_prompt, x_sample, cache_sb_k, cache_sb_v, state_ssd_conv, state_ssd,
           state_rwkv_shift, state_rwkv_wkv,
           l0_norm_mix, l0_w_in, l0_conv_w, l0_conv_b, l0_dt_bias, l0_a_log, l0_d_skip,
           l0_ssd_norm, l0_w_out,
           l1_norm_mix, l1_mu, l1_w_rkv, l1_w0, l1_w1, l1_w2, l1_a0, l1_a1, l1_a2,
           l1_g1, l1_g2, l1_k_k, l1_k_a, l1_r_k, l1_lnx_w, l1_lnx_b, l1_w_out,
           l0_norm_ffn, l0_peer_wq, l0_peer_keys, l0_peer_u, l0_peer_v,
           l1_norm_ffn, l1_peer_wq, l1_peer_keys, l1_peer_u, l1_peer_v,
           final_norm):
    layer_params = [
        (l0_norm_mix, l0_w_in, l0_conv_w, l0_conv_b, l0_dt_bias, l0_a_log, l0_d_skip,
         l0_ssd_norm, l0_w_out),
        (l1_norm_mix, l1_mu, l1_w_rkv, l1_w0, l1_w1, l1_w2, l1_a0, l1_a1, l1_a2,
         l1_g1, l1_g2, l1_k_k, l1_k_a, l1_r_k, l1_lnx_w, l1_lnx_b, l1_w_out),
    ]
    peer_params = [
        (l0_norm_ffn, l0_peer_wq, l0_peer_keys, pack_table(l0_peer_u), pack_table(l0_peer_v)),
        (l1_norm_ffn, l1_peer_wq, l1_peer_keys, pack_table(l1_peer_u), pack_table(l1_peer_v)),
    ]
    bp = x_prompt.shape[0]
    dt = x_prompt.dtype
    zero_states = [
        (jnp.zeros((bp, 0, SB_HEADS, SB_HEAD_DIM), dt), jnp.zeros((bp, 0, SB_HEADS, SB_HEAD_DIM), dt),
         jnp.zeros((bp, SSD_CONV - 1, SSD_CONV_DIM), dt),
         jnp.zeros((bp, SSD_HEADS, SSD_HEAD_DIM, SSD_STATE), dt)),
        (jnp.zeros((bp, 1, D_MODEL), dt), jnp.zeros((bp, RW_HEADS, RW_HEAD, RW_HEAD), dt)),
    ]
    sample_states = [
        (cache_sb_k, cache_sb_v, state_ssd_conv, state_ssd),
        (state_rwkv_shift, state_rwkv_wkv),
    ]
    (y_prompt, y_sample), (p_states, s_states) = run_trunks(
        [x_prompt, x_sample], [zero_states, sample_states], layer_params, peer_params, final_norm)
    p_sb_k, p_sb_v, p_conv, p_ssd, p_shift, p_wkv = p_states
    s_sb_k, s_sb_v, s_conv, s_ssd, s_shift, s_wkv = s_states
    return (y_prompt, y_sample, p_sb_k, p_sb_v, p_conv, p_ssd, p_shift, p_wkv,
            s_sb_k, s_sb_v, s_conv, s_ssd, s_shift, s_wkv)
```

```python
import functools
import math
import jax, jax.numpy as jnp
from jax import lax
import numpy as np
from jax.experimental import pallas as pl
from jax.experimental.pallas import tpu as pltpu

D_MODEL = 1024
DEPTH = 2

CHUNK = 64
EPS = 1e-6
SB_HEADS = 8
SB_HEAD_DIM = 64
SB_WIDTH = SB_HEADS * SB_HEAD_DIM
SB_BLOCK = 128
SSD_HEADS = 8
SSD_HEAD_DIM = 64
SSD_INNER = SSD_HEADS * SSD_HEAD_DIM
SSD_GROUPS = 2
SSD_STATE = 64
SSD_CONV = 4
SSD_CONV_DIM = SSD_INNER + 2 * SSD_GROUPS * SSD_STATE
SSD_CHUNK = CHUNK
HYB_SPLITS = (SB_WIDTH, 2 * SB_WIDTH, 3 * SB_WIDTH, 3 * SB_WIDTH + SSD_INNER,
              3 * SB_WIDTH + SSD_INNER + SSD_CONV_DIM)
HYB_IN = 3 * SB_WIDTH + SSD_INNER + SSD_CONV_DIM + SSD_HEADS
RW_HEAD = 64
RW_HEADS = D_MODEL // RW_HEAD
RW_LNX_EPS = 64e-5
PEER_HEADS = 8
PEER_NKEYS = 128
PEER_QDIM = 256
PEER_TOPK = 16
PEER_BLOCK = 128

F32 = jnp.float32
V7X_VMEM_BYTES = 64 * 1024 * 1024


def _cparams(*semantics):
    return pltpu.CompilerParams(dimension_semantics=semantics, vmem_limit_bytes=V7X_VMEM_BYTES)


def rmsnorm(x, g):
    xf = x.astype(F32)
    y = xf * lax.rsqrt(jnp.mean(xf * xf, axis=-1, keepdims=True) + EPS)
    return (y * g.astype(F32)).astype(x.dtype)


def _rmsnorm_kernel(x_ref, g_ref, o_ref):
    x = x_ref[...]
    y = x * lax.rsqrt(jnp.mean(x * x, axis=-1, keepdims=True) + EPS)
    o_ref[...] = y * g_ref[...]


def pallas_rmsnorm(x, g, rows=256):
    shp = x.shape
    x2 = x.reshape(-1, shp[-1])
    t, d = x2.shape
    rows = min(rows, t)
    out = pl.pallas_call(
        _rmsnorm_kernel,
        grid=(t // rows,),
        in_specs=[pl.BlockSpec((rows, d), lambda i: (i, 0)),
                  pl.BlockSpec((1, d), lambda i: (0, 0))],
        out_specs=pl.BlockSpec((rows, d), lambda i: (i, 0)),
        out_shape=jax.ShapeDtypeStruct((t, d), x.dtype),
        compiler_params=_cparams("parallel"),
        name="final_rmsnorm",
    )(x2, g.reshape(1, d))
    return out.reshape(shp)


SB_KBLOCK = 128
SB_LOG_FLOOR = -90.0


def _split3(x):
    h = x.astype(jnp.bfloat16)
    r = x - h.astype(F32)
    m = r.astype(jnp.bfloat16)
    l = (r - m.astype(F32)).astype(jnp.bfloat16)
    return h, m, l


def _sb_attn_kernel(q_ref, k_ref, v_ref, o_ref, *, tq, past):
    iq = pl.program_id(2)
    q = q_ref[0, 0]
    q_pos = past + iq * tq + lax.broadcasted_iota(jnp.int32, (tq, SB_KBLOCK), 0)
    k_off = lax.broadcasted_iota(jnp.int32, (tq, SB_KBLOCK), 1)
    r = lax.broadcasted_iota(jnp.int32, (SB_KBLOCK, SB_KBLOCK), 0)
    c = lax.broadcasted_iota(jnp.int32, (SB_KBLOCK, SB_KBLOCK), 1)
    upper = (r > c).astype(jnp.bfloat16)
    j0 = (past + (iq + 1) * tq - 1) // SB_KBLOCK

    def cond(state):
        j, live, _, _ = state
        return jnp.logical_and(j >= 0, live > 0)

    def body(state):
        j, _, carry, acc = state
        start = pl.multiple_of(j * SB_KBLOCK, SB_KBLOCK)
        kb = k_ref[0, 0, pl.ds(start, SB_KBLOCK), :]
        vb = v_ref[0, 0, pl.ds(start, SB_KBLOCK), :]
        z = lax.dot_general(q, kb, (((1,), (1,)), ((), ())), preferred_element_type=F32)
        mask = (j * SB_KBLOCK + k_off) < q_pos
        sp = jnp.maximum(z, 0.0) + jnp.log1p(jnp.exp(-jnp.abs(z)))
        log_fail = jnp.where(mask, -sp, 0.0)
        h, m, l = _split3(log_fail)
        later = (jnp.dot(h, upper, preferred_element_type=F32)
                 + jnp.dot(m, upper, preferred_element_type=F32)
                 + jnp.dot(l, upper, preferred_element_type=F32))
        log_w = (z - sp) + later + carry
        w = jnp.where(mask, jnp.exp(log_w), 0.0)
        acc = acc + jnp.dot(w.astype(jnp.bfloat16), vb, preferred_element_type=F32)
        carry = carry + jnp.sum(log_fail, axis=-1, keepdims=True)
        live = (jnp.max(carry) > SB_LOG_FLOOR).astype(jnp.int32)
        return j - 1, live, carry, acc

    state = (j0, jnp.int32(1), jnp.zeros((tq, 1), F32), jnp.zeros((tq, SB_HEAD_DIM), F32))
    _, _, _, acc = lax.while_loop(cond, body, state)
    o_ref[0, 0] = acc


def sb_attention(q, k_all, v_all, past):
    b, lq, h, d = q.shape
    s = k_all.shape[1]
    tq = min(lq, 128)
    s_pad = -(-s // SB_KBLOCK) * SB_KBLOCK
    qh = (q * (d ** -0.5)).astype(jnp.bfloat16).transpose(0, 2, 1, 3)
    kh = jnp.pad(k_all.astype(jnp.bfloat16).transpose(0, 2, 1, 3), ((0, 0), (0, 0), (0, s_pad - s), (0, 0)))
    vh = jnp.pad(v_all.astype(jnp.bfloat16).transpose(0, 2, 1, 3), ((0, 0), (0, 0), (0, s_pad - s), (0, 0)))
    out = pl.pallas_call(
        functools.partial(_sb_attn_kernel, tq=tq, past=past),
        grid=(b, h, lq // tq),
        in_specs=[pl.BlockSpec((1, 1, tq, d), lambda bi, hi, qi: (bi, hi, qi, 0)),
                  pl.BlockSpec((1, 1, s_pad, d), lambda bi, hi, qi: (bi, hi, 0, 0)),
                  pl.BlockSpec((1, 1, s_pad, d), lambda bi, hi, qi: (bi, hi, 0, 0))],
        out_specs=pl.BlockSpec((1, 1, tq, d), lambda bi, hi, qi: (bi, hi, qi, 0)),
        out_shape=jax.ShapeDtypeStruct((b, h, lq, d), F32),
        compiler_params=_cparams("parallel", "parallel", "arbitrary"),
        name="sb_attention",
    )(qh, kh, vh)
    return out.transpose(0, 2, 1, 3)


def causal_conv(u, buf, w, bias):
    L = u.shape[1]
    full = jnp.concatenate([buf, u], axis=1)
    out = bias
    for i in range(SSD_CONV):
        out = out + full[:, i:i + L] * w[i]
    return out, full[:, -(SSD_CONV - 1):]


def ssd_scan(x, dt, a, bm, cm, h0):
    b, L, H, P = x.shape
    Q = SSD_CHUNK if L % SSD_CHUNK == 0 else L
    nc = L // Q
    rep = H // SSD_GROUPS
    xf = x.astype(F32).reshape(b, nc, Q, H, P)
    bh = jnp.repeat(bm.astype(F32), rep, axis=2).reshape(b, nc, Q, H, SSD_STATE)
    ch = jnp.repeat(cm.astype(F32), rep, axis=2).reshape(b, nc, Q, H, SSD_STATE)
    dtc = dt.reshape(b, nc, Q, H)
    a_cum = jnp.cumsum(dtc * a, axis=2)
    seg = a_cum[:, :, :, None, :] - a_cum[:, :, None, :, :]
    causal = jnp.tril(jnp.ones((Q, Q), dtype=bool))[None, None, :, :, None]
    decay = jnp.exp(jnp.where(causal, seg, -jnp.inf))
    cb = jnp.einsum('bclhn,bcshn->bclsh', ch, bh)
    y_diag = jnp.einsum('bclsh,bcshp->bclhp', cb * decay * dtc[:, :, None], xf)
    w_state = jnp.exp(a_cum[:, :, -1:] - a_cum) * dtc
    chunk_states = jnp.einsum('bclhn,bclh,bclhp->bchpn', bh, w_state, xf)
    chunk_decay = jnp.exp(a_cum[:, :, -1])

    def step(h, inp):
        dec, st = inp
        return dec[:, :, None, None] * h + st, h

    h_last, h_prev = lax.scan(step, h0, (chunk_decay.transpose(1, 0, 2),
                                          chunk_states.transpose(1, 0, 2, 3, 4)))
    h_prev = h_prev.transpose(1, 0, 2, 3, 4)
    y_off = jnp.einsum('bclhn,bchpn->bclhp', ch, h_prev) * jnp.exp(a_cum)[..., None]
    return (y_diag + y_off).reshape(b, L, H, P), h_last


def hybrid_mixer(xn, k_cache, v_cache, conv_buf, ssd_h,
                 w_in, conv_w, conv_b, dt_bias, a_log, d_skip, ssd_norm, w_out):
    b, L, _ = xn.shape
    proj = xn @ w_in
    q, k, v, z, xbc, dt = jnp.split(proj, HYB_SPLITS, axis=-1)
    q = q.reshape(b, L, SB_HEADS, SB_HEAD_DIM)
    k = k.reshape(b, L, SB_HEADS, SB_HEAD_DIM)
    v = v.reshape(b, L, SB_HEADS, SB_HEAD_DIM)
    past = k_cache.shape[1]
    k_all = jnp.concatenate([k_cache, k], axis=1)
    v_all = jnp.concatenate([v_cache, v], axis=1)
    o_sb = sb_attention(q, k_all, v_all, past).reshape(b, L, SB_WIDTH)
    xbc_c, new_buf = causal_conv(xbc, conv_buf, conv_w, conv_b)
    xbc_c = jax.nn.silu(xbc_c)
    xs, bm, cm = jnp.split(xbc_c, (SSD_INNER, SSD_INNER + SSD_GROUPS * SSD_STATE), axis=-1)
    xs = xs.reshape(b, L, SSD_HEADS, SSD_HEAD_DIM)
    bm = bm.reshape(b, L, SSD_GROUPS, SSD_STATE)
    cm = cm.reshape(b, L, SSD_GROUPS, SSD_STATE)
    dtp = jax.nn.softplus(dt.astype(F32) + dt_bias.astype(F32))
    a = -jnp.exp(a_log.astype(F32))
    y, h_new = ssd_scan(xs, dtp, a, bm, cm, ssd_h.astype(F32))
    y = y + d_skip.astype(F32)[:, None] * xs.astype(F32)
    y = y.reshape(b, L, SSD_INNER) * jax.nn.silu(z.astype(F32))
    yg = y.reshape(b, L, SSD_GROUPS, SSD_INNER // SSD_GROUPS)
    yg = yg * lax.rsqrt(jnp.mean(yg * yg, axis=-1, keepdims=True) + EPS)
    y = yg.reshape(b, L, SSD_INNER) * ssd_norm.astype(F32)
    out = jnp.concatenate([o_sb, y.astype(xn.dtype)], axis=-1) @ w_out
    return out, k, v, new_buf, h_new.astype(ssd_h.dtype)


RW_PAIRS = RW_HEADS // 2
RW_NP = 4
RW_TC = 64


def _rwkv_scan_kernel(r_ref, d_ref, k_ref, v_ref, a_ref, b_ref, s0_ref, o_ref, st_ref, s_scr, vcol_scr, *, tc):
    c = pl.program_id(2)

    @pl.when(c == 0)
    def _():
        s_scr[...] = s0_ref[0]

    row = lax.broadcasted_iota(jnp.int32, (RW_HEAD, 128), 0)
    lane = lax.broadcasted_iota(jnp.int32, (RW_HEAD, 128), 1)
    left = lane < RW_HEAD
    diag = (lane & (RW_HEAD - 1)) == row
    same_head = ((lax.broadcasted_iota(jnp.int32, (128, 128), 0) < RW_HEAD)
                 == (lax.broadcasted_iota(jnp.int32, (128, 128), 1) < RW_HEAD)).astype(jnp.bfloat16)
    same_head2 = jnp.concatenate([same_head] * 2, axis=0)

    def half_sums(x):
        lo = jnp.sum(jnp.where(left, x, 0.0), axis=1, keepdims=True)
        hi = jnp.sum(jnp.where(left, 0.0, x), axis=1, keepdims=True)
        return jnp.where(left, lo, hi)

    def vcol_body(t8, carry):
        base = pl.multiple_of(t8 * 8, 8)
        for p in range(RW_NP):
            terms = _terms3(v_ref[0, pl.ds(base, 8), p * 128:(p + 1) * 128])[:2]
            for i in range(8):
                h, m = (jnp.where(diag, t[i:i + 1], 0.0).astype(jnp.bfloat16) for t in terms)
                vcol_scr[p, base + i] = jnp.dot(jnp.concatenate([h, m], axis=1), same_head2,
                                                preferred_element_type=F32)
        return carry

    lax.fori_loop(0, tc // 8, vcol_body, 0)

    def step8(t8, carry):
        base = pl.multiple_of(t8 * 8, 8)
        lanes = [slice(p * 128, (p + 1) * 128) for p in range(RW_NP)]
        blk = [[ref[0, pl.ds(base, 8), sl] for ref in (a_ref, d_ref, b_ref, k_ref, r_ref)] for sl in lanes]
        state = [s_scr[p] for p in range(RW_NP)]
        rows = [[] for _ in range(RW_NP)]
        for i in range(8):
            for p in range(RW_NP):
                a8, d8, b8, k8, r8 = blk[p]
                s = state[p]
                sa = half_sums(s * a8[i:i + 1])
                s = s * d8[i:i + 1] + sa * b8[i:i + 1] + vcol_scr[p, base + i] * k8[i:i + 1]
                state[p] = s
                h, m, _ = _split3(s * r8[i:i + 1])
                oc = jnp.dot(jnp.concatenate([h, m], axis=1), same_head2, preferred_element_type=F32)
                rows[p].append(jnp.sum(jnp.where(diag, oc, 0.0), axis=0, keepdims=True))
        for p in range(RW_NP):
            s_scr[p] = state[p]
            o_ref[0, pl.ds(base, 8), lanes[p]] = jnp.concatenate(rows[p], axis=0)
        return carry

    lax.fori_loop(0, tc // 8, step8, 0)

    @pl.when(c == pl.num_programs(2) - 1)
    def _():
        st_ref[0] = s_scr[...]


def rwkv_scan(r, d, k, v, a, b, wkv):
    bsz, L, D = r.shape
    tc = min(RW_TC, L)
    s0 = wkv.reshape(bsz, RW_PAIRS, 2, RW_HEAD, RW_HEAD).transpose(0, 1, 3, 2, 4).reshape(bsz, RW_PAIRS, RW_HEAD, 128)
    seq = pl.BlockSpec((1, tc, RW_NP * 128), lambda bi, g, c: (bi, c, g))
    st = pl.BlockSpec((1, RW_NP, RW_HEAD, 128), lambda bi, g, c: (bi, g, 0, 0))
    o, s_new = pl.pallas_call(
        functools.partial(_rwkv_scan_kernel, tc=tc),
        grid=(bsz, RW_PAIRS // RW_NP, L // tc),
        in_specs=[seq] * 6 + [st],
        out_specs=[seq, st],
        out_shape=[jax.ShapeDtypeStruct((bsz, L, D), F32),
                   jax.ShapeDtypeStruct((bsz, RW_PAIRS, RW_HEAD, 128), F32)],
        scratch_shapes=[pltpu.VMEM((RW_NP, RW_HEAD, 128), F32),
                        pltpu.VMEM((RW_NP, tc, RW_HEAD, 128), F32)],
        compiler_params=_cparams("parallel", "parallel", "arbitrary"),
        name="rwkv_scan",
    )(r, d, k, v, a, b, s0)
    s_new = s_new.reshape(bsz, RW_PAIRS, RW_HEAD, 2, RW_HEAD).transpose(0, 1, 3, 2, 4).reshape(bsz, RW_HEADS, RW_HEAD, RW_HEAD)
    return o, s_new


def _proj3_kernel(x_ref, w_ref, o_ref):
    o_ref[0] = jnp.dot(x_ref[0], w_ref[0], preferred_element_type=F32)


def proj3(x3, w3):
    _, m, kdim = x3.shape
    n = w3.shape[2]
    tm = min(m, 512)
    tn = min(n, 512)
    return pl.pallas_call(
        _proj3_kernel,
        grid=(3, m // tm, n // tn),
        in_specs=[pl.BlockSpec((1, tm, kdim), lambda i, a, b: (i, a, 0)),
                  pl.BlockSpec((1, kdim, tn), lambda i, a, b: (i, 0, b))],
        out_specs=pl.BlockSpec((1, tm, tn), lambda i, a, b: (i, a, b)),
        out_shape=jax.ShapeDtypeStruct((3, m, n), F32),
        compiler_params=_cparams("parallel", "parallel", "parallel"),
        name="rwkv_proj3",
    )(x3, w3)


def rwkv7_mixer(xn, shift_buf, wkv, mu, w_rkv, w0, w1, w2, a0, a1, a2, g1, g2,
                k_k, k_a, r_k, lnx_w, lnx_b, w_out):
    b, L, D = xn.shape
    x_prev = jnp.concatenate([shift_buf, xn[:, :-1]], axis=1)
    xx = x_prev - xn
    xm = xn[None] + xx[None] * mu[:, None, None, :]
    r, k, v = proj3(xm[:3].astype(jnp.bfloat16).reshape(3, b * L, D),
                    w_rkv.astype(jnp.bfloat16)).reshape(3, b, L, D)
    xw, xa, xg = xm[3], xm[4], xm[5]
    w = -jax.nn.softplus(-(w0 + jnp.tanh(xw @ w1) @ w2).astype(F32)) - 0.5
    decay = jnp.exp(-jnp.exp(w))
    a = jax.nn.sigmoid((a0 + (xa @ a1) @ a2).astype(F32))
    g = (jax.nn.sigmoid(xg @ g1) @ g2).astype(F32)
    hs = (b, L, RW_HEADS, RW_HEAD)
    kk = (k * k_k.astype(F32)).reshape(hs)
    kk = kk / jnp.maximum(jnp.sqrt(jnp.sum(kk * kk, axis=-1, keepdims=True)), 1e-12)
    k = k * (1.0 + (a - 1.0) * k_a.astype(F32))
    rh, dh, kh, vh, ah = (t.reshape(hs) for t in (r, decay, k, v, a))
    a_vec = -kk
    b_vec = kk * ah
    o, S_last = rwkv_scan(r, decay, k, v, a_vec.reshape(b, L, D), b_vec.reshape(b, L, D), wkv.astype(F32))
    o = o.reshape(hs)
    mean = jnp.mean(o, axis=-1, keepdims=True)
    var = jnp.mean(jnp.square(o - mean), axis=-1, keepdims=True)
    o = ((o - mean) * lax.rsqrt(var + RW_LNX_EPS)).reshape(b, L, D)
    o = o * lnx_w.astype(F32) + lnx_b.astype(F32)
    bonus = jnp.sum(rh * kh * r_k.astype(F32), axis=-1, keepdims=True) * vh
    o = o + bonus.reshape(b, L, D)
    out = (o * g).astype(xn.dtype) @ w_out
    return out, xn[:, -1:], S_last.astype(wkv.dtype)


PEER_TM = 256
PEER_TB = 64
PEER_NSTAGE = 16
PEER_HALF = D_MODEL // 2
PEER_SUB = PEER_HALF // 128
PEER_SEL = PEER_HEADS * PEER_TOPK


def _peer_scores_kernel(x_ref, g_ref, wq_ref, keys_ref, s_ref, xn_ref, xnb_ref):
    @pl.when(pl.program_id(1) == 0)
    def _():
        x = x_ref[...]
        xn = x * lax.rsqrt(jnp.mean(x * x, axis=-1, keepdims=True) + EPS) * g_ref[...]
        xn_ref[...] = xn
        xnb_ref[...] = xn.astype(jnp.bfloat16)

    q = jnp.dot(xnb_ref[...], wq_ref[...], preferred_element_type=F32).astype(jnp.bfloat16)
    half = PEER_QDIM // 2
    for i in range(2):
        s_ref[i * PEER_NKEYS:(i + 1) * PEER_NKEYS, :] = lax.dot_general(
            keys_ref[0, i], q[:, i * half:(i + 1) * half],
            (((1,), (1,)), ((), ())), preferred_element_type=F32)


def peer_scores(x, g, wq_bf, keys_bf):
    t, d = x.shape
    return pl.pallas_call(
        _peer_scores_kernel,
        grid=(t // PEER_TM, PEER_HEADS),
        in_specs=[pl.BlockSpec((PEER_TM, d), lambda i, h: (i, 0)),
                  pl.BlockSpec((1, d), lambda i, h: (0, 0)),
                  pl.BlockSpec((d, PEER_QDIM), lambda i, h: (0, h)),
                  pl.BlockSpec((1, 2, PEER_NKEYS, PEER_QDIM // 2), lambda i, h: (h, 0, 0, 0))],
        out_specs=[pl.BlockSpec((2 * PEER_NKEYS, PEER_TM), lambda i, h: (h, i)),
                   pl.BlockSpec((PEER_TM, d), lambda i, h: (i, 0))],
        out_shape=[jax.ShapeDtypeStruct((PEER_HEADS * 2 * PEER_NKEYS, t), F32),
                   jax.ShapeDtypeStruct((t, d), F32)],
        scratch_shapes=[pltpu.VMEM((PEER_TM, d), jnp.bfloat16)],
        compiler_params=_cparams("parallel", "arbitrary"),
        name="peer_scores",
    )(x, g.reshape(1, d), wq_bf, keys_bf)


PEER_TT = 128
NEG_INF = float("-inf")


def _peer_candidates():
    k = PEER_TOPK
    a_idx, b_idx, valid = [], [], []
    for a in range(k // 2):
        nb = k // (a + 1)
        rows = k if a == 0 else 8
        for b in range(rows):
            a_idx.append(a); b_idx.append(b); valid.append(b < nb)
    for a in range(k // 2, k):
        a_idx.append(a); b_idx.append(0); valid.append(True)
    return np.array(a_idx), np.array(b_idx), np.array(valid)


_CAND_A, _CAND_B, _CAND_VALID = _peer_candidates()
PEER_NCAND = len(_CAND_A)


def _top_rows(xs, tag, k):
    big = jnp.float32(1e9)
    xs = list(xs)
    vals = [[] for _ in xs]
    tags = [[] for _ in xs]
    for _ in range(k):
        for n, x in enumerate(xs):
            m = jnp.max(x, axis=0, keepdims=True)
            t = jnp.min(jnp.where(x == m, tag, big), axis=0, keepdims=True)
            vals[n].append(m)
            tags[n].append(t)
            xs[n] = jnp.where(tag == t, NEG_INF, x)
    return vals, tags


def _peer_topk_kernel(s_ref, pos_ref, eid_ref, gate_ref, v_scr, i_scr, e_scr, g_scr):
    k = PEER_TOPK
    key_tag = lax.broadcasted_iota(jnp.int32, (PEER_NKEYS, PEER_TT), 0).astype(F32)

    def half_body(h, carry):
        starts = [pl.multiple_of((2 * h + i) * PEER_NKEYS, PEER_NKEYS) for i in range(2)]
        vals, tags = _top_rows([s_ref[pl.ds(st, PEER_NKEYS), :] for st in starts], key_tag, k)
        for i in range(2):
            v_scr[2 * h + i] = jnp.concatenate(vals[i], axis=0)
            i_scr[2 * h + i] = jnp.concatenate(tags[i], axis=0)
        return carry

    lax.fori_loop(0, PEER_HEADS, half_body, 0)

    pos = pos_ref[...]
    valid = pos >= 0.0

    def candidates(h):
        s1 = v_scr[2 * h]; s2 = v_scr[2 * h + 1]
        i1 = i_scr[2 * h]; i2 = i_scr[2 * h + 1]
        e2 = i2[0:8]
        pieces_s = [s1[0:1] + s2]
        pieces_e = [i1[0:1] * PEER_NKEYS + i2]
        for a in range(1, k // 2):
            pieces_s.append(s1[a:a + 1] + s2[0:8])
            pieces_e.append(i1[a:a + 1] * PEER_NKEYS + e2)
        pieces_s.append(s1[k // 2:k] + s2[0:1])
        pieces_e.append(i1[k // 2:k] * PEER_NKEYS + i2[0:1])
        return jnp.where(valid, jnp.concatenate(pieces_s, axis=0), NEG_INF), jnp.concatenate(pieces_e, axis=0)

    def head_pair_body(hp, carry):
        heads = [2 * hp, 2 * hp + 1]
        cands, cids = zip(*[candidates(h) for h in heads])
        tops, tags = _top_rows(cands, pos, k)
        for n, h in enumerate(heads):
            eids = [jnp.max(jnp.where(pos == t, cids[n], -1.0), axis=0, keepdims=True) for t in tags[n]]
            top = jnp.concatenate(tops[n], axis=0)
            p = jnp.exp(top - top[0:1])
            gate = p / jnp.sum(p, axis=0, keepdims=True)
            row = pl.multiple_of(h * k, k)
            e_scr[pl.ds(row, k), :] = jnp.concatenate(eids, axis=0)
            g_scr[pl.ds(row, k), :] = gate
        return carry

    lax.fori_loop(0, PEER_HEADS // 2, head_pair_body, 0)
    eid_ref[...] = e_scr[...].T
    gate_ref[...] = g_scr[...].T


def peer_topk(scores_t):
    t = scores_t.shape[1]
    pos = np.where(_CAND_VALID, _CAND_A * PEER_TOPK + _CAND_B, -1).astype(np.float32)
    pos = jnp.asarray(np.broadcast_to(pos[:, None], (PEER_NCAND, PEER_TT)))
    return pl.pallas_call(
        _peer_topk_kernel,
        grid=(t // PEER_TT,),
        in_specs=[pl.BlockSpec((2 * PEER_HEADS * PEER_NKEYS, PEER_TT), lambda i: (0, i)),
                  pl.BlockSpec((PEER_NCAND, PEER_TT), lambda i: (0, 0))],
        out_specs=[pl.BlockSpec((PEER_TT, PEER_SEL), lambda i: (i, 0)),
                   pl.BlockSpec((PEER_TT, PEER_SEL), lambda i: (i, 0))],
        out_shape=[jax.ShapeDtypeStruct((t, PEER_SEL), F32),
                   jax.ShapeDtypeStruct((t, PEER_SEL), F32)],
        scratch_shapes=[pltpu.VMEM((2 * PEER_HEADS, PEER_TOPK, PEER_TT), F32),
                        pltpu.VMEM((2 * PEER_HEADS, PEER_TOPK, PEER_TT), F32),
                        pltpu.VMEM((PEER_SEL, PEER_TT), F32),
                        pltpu.VMEM((PEER_SEL, PEER_TT), F32)],
        compiler_params=_cparams("parallel"),
        name="peer_topk",
    )(scores_t, pos)


def pack_table(tab):
    e = tab.shape[0]
    b = lax.bitcast_convert_type(tab.astype(jnp.bfloat16), jnp.uint16).astype(jnp.uint32)
    b = b.reshape(e, 2, PEER_SUB, 128)
    return (b[:, 0] << 16) | b[:, 1]


def _terms3(x):
    h = x.astype(jnp.bfloat16).astype(F32)
    r = x - h
    m = r.astype(jnp.bfloat16).astype(F32)
    l = (r - m).astype(jnp.bfloat16).astype(F32)
    return h, m, l


def _stack_terms(terms, s):
    row = lax.broadcasted_iota(jnp.int32, (8, 128), 0)
    h, m, l = (t[s:s + 1] for t in terms)
    return jnp.where(row == 0, h, jnp.where(row == 1, m, jnp.where(row == 2, l, 0.0))).astype(jnp.bfloat16)


def _stage_rows(eid_ref, tab_ref, w_scr, t):
    for j in range(PEER_SEL):
        w_scr[PEER_SUB * j:PEER_SUB * (j + 1), :] = tab_ref[eid_ref[t, j]]


def _token_groups(eid_ref, tab_ref, w_scrs, compute):
    n = len(w_scrs)

    def group(i, carry):
        for k in range(n):
            _stage_rows(eid_ref, tab_ref, w_scrs[k], n * i + k)
        for k in range(n):
            compute(n * i + k, w_scrs[k])
        return carry

    lax.fori_loop(0, PEER_TB // n, group, 0)


def _halves(w_scr, s):
    w = w_scr[pl.ds(s, PEER_SEL, stride=PEER_SUB), :]
    hi = lax.bitcast_convert_type(w & jnp.uint32(0xFFFF0000), F32).astype(jnp.bfloat16)
    lo = lax.bitcast_convert_type(w << 16, F32).astype(jnp.bfloat16)
    return hi, lo


def _peer_hidden_kernel(eid_ref, x_ref, tab_ref, hid_ref, *w_scrs):
    nt = (((1,), (1,)), ((), ()))

    def compute(t, w_scr):
        th = _terms3(x_ref[t, 0])
        tl = _terms3(x_ref[t, 1])
        acc = jnp.zeros((8, PEER_SEL), F32)
        for s in range(PEER_SUB):
            hi, lo = _halves(w_scr, s)
            acc = acc + lax.dot_general(_stack_terms(th, s), hi, nt, preferred_element_type=F32)
            acc = acc + lax.dot_general(_stack_terms(tl, s), lo, nt, preferred_element_type=F32)
        hid_ref[t] = acc[0:1] + acc[1:2] + acc[2:3]

    _token_groups(eid_ref, tab_ref, w_scrs, compute)


def _peer_out_kernel(eid_ref, act_ref, res_ref, tab_ref, out_ref, *w_scrs):
    def compute(t, w_scr):
        a8 = _stack_terms(_terms3(act_ref[t]), 0)
        for s in range(PEER_SUB):
            hi, lo = _halves(w_scr, s)
            yh = jnp.dot(a8, hi, preferred_element_type=F32)
            yl = jnp.dot(a8, lo, preferred_element_type=F32)
            out_ref[t, 0, s:s + 1, :] = res_ref[t, 0, s:s + 1, :] + (yh[0:1] + yh[1:2] + yh[2:3])
            out_ref[t, 1, s:s + 1, :] = res_ref[t, 1, s:s + 1, :] + (yl[0:1] + yl[1:2] + yl[2:3])

    _token_groups(eid_ref, tab_ref, w_scrs, compute)


def _table_spec(n_exp):
    return pl.BlockSpec((n_exp, PEER_SUB, 128), lambda i: (0, 0, 0), pipeline_mode=pl.Buffered(1))


def peer_hidden(eid, xn_split, tab_packed):
    t = eid.shape[0]
    n_exp = tab_packed.shape[0]
    return pl.pallas_call(
        _peer_hidden_kernel,
        grid=(t // PEER_TB,),
        in_specs=[pl.BlockSpec((PEER_TB, PEER_SEL), lambda i: (i, 0), memory_space=pltpu.SMEM),
                  pl.BlockSpec((PEER_TB, 2, PEER_SUB, 128), lambda i: (i, 0, 0, 0)),
                  _table_spec(n_exp)],
        out_specs=pl.BlockSpec((PEER_TB, 1, PEER_SEL), lambda i: (i, 0, 0)),
        out_shape=jax.ShapeDtypeStruct((t, 1, PEER_SEL), F32),
        scratch_shapes=[pltpu.VMEM((PEER_SEL * PEER_SUB, 128), jnp.uint32)] * PEER_NSTAGE,
        compiler_params=_cparams("parallel"),
        name="peer_hidden",
    )(eid, xn_split, tab_packed)


def peer_out(eid, act, res_split, tab_packed):
    t = eid.shape[0]
    n_exp = tab_packed.shape[0]
    return pl.pallas_call(
        _peer_out_kernel,
        grid=(t // PEER_TB,),
        in_specs=[pl.BlockSpec((PEER_TB, PEER_SEL), lambda i: (i, 0), memory_space=pltpu.SMEM),
                  pl.BlockSpec((PEER_TB, 1, PEER_SEL), lambda i: (i, 0, 0)),
                  pl.BlockSpec((PEER_TB, 2, PEER_SUB, 128), lambda i: (i, 0, 0, 0)),
                  _table_spec(n_exp)],
        out_specs=pl.BlockSpec((PEER_TB, 2, PEER_SUB, 128), lambda i: (i, 0, 0, 0)),
        out_shape=jax.ShapeDtypeStruct((t, 2, PEER_SUB, 128), F32),
        scratch_shapes=[pltpu.VMEM((PEER_SEL * PEER_SUB, 128), jnp.uint32)] * PEER_NSTAGE,
        compiler_params=_cparams("parallel"),
        name="peer_out",
    )(eid, act.reshape(t, 1, PEER_SEL), res_split, tab_packed)


def peer_layer(x, ng, w_q, keys, u_packed, v_packed):
    t, d = x.shape
    scores, xn = peer_scores(x, ng, w_q.astype(jnp.bfloat16), keys.astype(jnp.bfloat16))
    eid_f, gate = peer_topk(scores)
    eid = eid_f.astype(jnp.int32)
    hidden = peer_hidden(eid, xn.reshape(t, 2, PEER_SUB, 128), u_packed).reshape(t, PEER_SEL)
    act = jax.nn.gelu(hidden, approximate=False) * gate
    out = peer_out(eid, act, x.reshape(t, 2, PEER_SUB, 128), v_packed)
    return out.reshape(t, d)


def mixer_layer(l, x, states, mp):
    xn = rmsnorm(x, mp[0])
    if l % 2 == 0:
        out, *ns = hybrid_mixer(xn, *states, *mp[1:])
    else:
        out, *ns = rwkv7_mixer(xn, *states, *mp[1:])
    return x + out, ns


def run_trunks(xs, states_list, layer_params, peer_params, final_norm):
    new_states = [[] for _ in xs]
    sizes = [x.shape[0] * x.shape[1] for x in xs]
    offs = np.cumsum([0] + sizes)
    for l in range(DEPTH):
        mixed = []
        for gi, x in enumerate(xs):
            x, ns = mixer_layer(l, x, states_list[gi][l], layer_params[l])
            new_states[gi].extend(ns)
            mixed.append(x.reshape(-1, D_MODEL))
        ng, wq, keys, u_packed, v_packed = peer_params[l]
        y = peer_layer(jnp.concatenate(mixed, axis=0), ng, wq, keys, u_packed, v_packed)
        xs = [y[offs[gi]:offs[gi + 1]].reshape(xs[gi].shape) for gi in range(len(xs))]
    ys = [pallas_rmsnorm(x, final_norm) for x in xs]
    return ys, new_states


def kernel(x_prompt, x_sample, cache_sb_k, cache_sb_v, state_ssd_conv, state_ssd,
           state_rwkv_shift, state_rwkv_wkv,
           l0_norm_mix, l0_w_in, l0_conv_w, l0_conv_b, l0_dt_bias, l0_a_log, l0_d_skip,
           l0_ssd_norm, l0_w_out,
           l1_norm_mix, l1_mu, l1_w_rkv, l1_w0, l1_w1, l1_w2, l1_a0, l1_a1, l1_a2,
           l1_g1, l1_g2, l1_k_k, l1_k_a, l1_r_k, l1_lnx_w, l1_lnx_b, l1_w_out,
           l0_norm_ffn, l0_peer_wq, l0_peer_keys, l0_peer_u, l0_peer_v,
           l1_norm_ffn, l1_peer_wq, l1_peer_keys, l1_peer_u, l1_peer_v,
           final_norm):
    layer_params = [
        (l0_norm_mix, l0_w_in, l0_conv_w, l0_conv_b, l0_dt_bias, l0_a_log, l0_d_skip,
         l0_ssd_norm, l0_w_out),
        (l1_norm_mix, l1_mu, l1_w_rkv, l1_w0, l1_w1, l1_w2, l1_a0, l1_a1, l1_a2,
         l1_g1, l1_g2, l1_k_k, l1_k_a, l1_r_k, l1_lnx_w, l1_lnx_b, l1_w_out),
    ]
    peer_params = [
        (l0_norm_ffn, l0_peer_wq, l0_peer_keys, pack_table(l0_peer_u), pack_table(l0_peer_v)),
        (l1_norm_ffn, l1_peer_wq, l1_peer_keys, pack_table(l1_peer_u), pack_table(l1_peer_v)),
    ]
    bp = x_prompt.shape[0]
    dt = x_prompt.dtype
    zero_states = [
        (jnp.zeros((bp, 0, SB_HEADS, SB_HEAD_DIM), dt), jnp.zeros((bp, 0, SB_HEADS, SB_HEAD_DIM), dt),
         jnp.zeros((bp, SSD_CONV - 1, SSD_CONV_DIM), dt),
         jnp.zeros((bp, SSD_HEADS, SSD_HEAD_DIM, SSD_STATE), dt)),
        (jnp.zeros((bp, 1, D_MODEL), dt), jnp.zeros((bp, RW_HEADS, RW_HEAD, RW_HEAD), dt)),
    ]
    sample_states = [
        (cache_sb_k, cache_sb_v, state_ssd_conv, state_ssd),
        (state_rwkv_shift, state_rwkv_wkv),
    ]
    (y_prompt, y_sample), (p_states, s_states) = run_trunks(
        [x_prompt, x_sample], [zero_states, sample_states], layer_params, peer_params, final_norm)
    p_sb_k, p_sb_v, p_conv, p_ssd, p_shift, p_wkv = p_states
    s_sb_k, s_sb_v, s_conv, s_ssd, s_shift, s_wkv = s_states
    return (y_prompt, y_sample, p_sb_k, p_sb_v, p_conv, p_ssd, p_shift, p_wkv,
            s_sb_k, s_sb_v, s_conv, s_ssd, s_shift, s_wkv)
```

```python
import functools
import math
import jax, jax.numpy as jnp
from jax import lax
import numpy as np
from jax.experimental import pallas as pl
from jax.experimental.pallas import tpu as pltpu

D_MODEL = 1024
DEPTH = 2

CHUNK = 64
EPS = 1e-6
SB_HEADS = 8
SB_HEAD_DIM = 64
SB_WIDTH = SB_HEADS * SB_HEAD_DIM
SB_BLOCK = 128
SSD_HEADS = 8
SSD_HEAD_DIM = 64
SSD_INNER = SSD_HEADS * SSD_HEAD_DIM
SSD_GROUPS = 2
SSD_STATE = 64
SSD_CONV = 4
SSD_CONV_DIM = SSD_INNER + 2 * SSD_GROUPS * SSD_STATE
SSD_CHUNK = CHUNK
HYB_SPLITS = (SB_WIDTH, 2 * SB_WIDTH, 3 * SB_WIDTH, 3 * SB_WIDTH + SSD_INNER,
              3 * SB_WIDTH + SSD_INNER + SSD_CONV_DIM)
HYB_IN = 3 * SB_WIDTH + SSD_INNER + SSD_CONV_DIM + SSD_HEADS
RW_HEAD = 64
RW_HEADS = D_MODEL // RW_HEAD
RW_LNX_EPS = 64e-5
PEER_HEADS = 8
PEER_NKEYS = 128
PEER_QDIM = 256
PEER_TOPK = 16
PEER_BLOCK = 128

F32 = jnp.float32
V7X_VMEM_BYTES = 64 * 1024 * 1024


def _cparams(*semantics):
    return pltpu.CompilerParams(dimension_semantics=semantics, vmem_limit_bytes=V7X_VMEM_BYTES)


def rmsnorm(x, g):
    xf = x.astype(F32)
    y = xf * lax.rsqrt(jnp.mean(xf * xf, axis=-1, keepdims=True) + EPS)
    return (y * g.astype(F32)).astype(x.dtype)


def _rmsnorm_kernel(x_ref, g_ref, o_ref):
    x = x_ref[...]
    y = x * lax.rsqrt(jnp.mean(x * x, axis=-1, keepdims=True) + EPS)
    o_ref[...] = y * g_ref[...]


def pallas_rmsnorm(x, g, rows=256):
    shp = x.shape
    x2 = x.reshape(-1, shp[-1])
    t, d = x2.shape
    rows = min(rows, t)
    out = pl.pallas_call(
        _rmsnorm_kernel,
        grid=(t // rows,),
        in_specs=[pl.BlockSpec((rows, d), lambda i: (i, 0)),
                  pl.BlockSpec((1, d), lambda i: (0, 0))],
        out_specs=pl.BlockSpec((rows, d), lambda i: (i, 0)),
        out_shape=jax.ShapeDtypeStruct((t, d), x.dtype),
        compiler_params=_cparams("parallel"),
        name="final_rmsnorm",
    )(x2, g.reshape(1, d))
    return out.reshape(shp)


SB_KBLOCK = 128
SB_LOG_FLOOR = -90.0


def _split3(x):
    h = x.astype(jnp.bfloat16)
    r = x - h.astype(F32)
    m = r.astype(jnp.bfloat16)
    l = (r - m.astype(F32)).astype(jnp.bfloat16)
    return h, m, l


def _sb_attn_kernel(q_ref, k_ref, v_ref, o_ref, *, tq, past):
    iq = pl.program_id(2)
    q = q_ref[0, 0]
    q_pos = past + iq * tq + lax.broadcasted_iota(jnp.int32, (tq, SB_KBLOCK), 0)
    k_off = lax.broadcasted_iota(jnp.int32, (tq, SB_KBLOCK), 1)
    r = lax.broadcasted_iota(jnp.int32, (SB_KBLOCK, SB_KBLOCK), 0)
    c = lax.broadcasted_iota(jnp.int32, (SB_KBLOCK, SB_KBLOCK), 1)
    upper = (r > c).astype(jnp.bfloat16)
    j0 = (past + (iq + 1) * tq - 1) // SB_KBLOCK

    def cond(state):
        j, live, _, _ = state
        return jnp.logical_and(j >= 0, live > 0)

    def body(state):
        j, _, carry, acc = state
        start = pl.multiple_of(j * SB_KBLOCK, SB_KBLOCK)
        kb = k_ref[0, 0, pl.ds(start, SB_KBLOCK), :]
        vb = v_ref[0, 0, pl.ds(start, SB_KBLOCK), :]
        z = lax.dot_general(q, kb, (((1,), (1,)), ((), ())), preferred_element_type=F32)
        mask = (j * SB_KBLOCK + k_off) < q_pos
        sp = jnp.maximum(z, 0.0) + jnp.log1p(jnp.exp(-jnp.abs(z)))
        log_fail = jnp.where(mask, -sp, 0.0)
        h, m, l = _split3(log_fail)
        later = (jnp.dot(h, upper, preferred_element_type=F32)
                 + jnp.dot(m, upper, preferred_element_type=F32)
                 + jnp.dot(l, upper, preferred_element_type=F32))
        log_w = (z - sp) + later + carry
        w = jnp.where(mask, jnp.exp(log_w), 0.0)
        acc = acc + jnp.dot(w.astype(jnp.bfloat16), vb, preferred_element_type=F32)
        carry = carry + jnp.sum(log_fail, axis=-1, keepdims=True)
        live = (jnp.max(carry) > SB_LOG_FLOOR).astype(jnp.int32)
        return j - 1, live, carry, acc

    state = (j0, jnp.int32(1), jnp.zeros((tq, 1), F32), jnp.zeros((tq, SB_HEAD_DIM), F32))
    _, _, _, acc = lax.while_loop(cond, body, state)
    o_ref[0, 0] = acc


def sb_attention(q, k_all, v_all, past):
    b, lq, h, d = q.shape
    s = k_all.shape[1]
    tq = min(lq, 128)
    s_pad = -(-s // SB_KBLOCK) * SB_KBLOCK
    qh = (q * (d ** -0.5)).astype(jnp.bfloat16).transpose(0, 2, 1, 3)
    kh = jnp.pad(k_all.astype(jnp.bfloat16).transpose(0, 2, 1, 3), ((0, 0), (0, 0), (0, s_pad - s), (0, 0)))
    vh = jnp.pad(v_all.astype(jnp.bfloat16).transpose(0, 2, 1, 3), ((0, 0), (0, 0), (0, s_pad - s), (0, 0)))
    out = pl.pallas_call(
        functools.partial(_sb_attn_kernel, tq=tq, past=past),
        grid=(b, h, lq // tq),
        in_specs=[pl.BlockSpec((1, 1, tq, d), lambda bi, hi, qi: (bi, hi, qi, 0)),
                  pl.BlockSpec((1, 1, s_pad, d), lambda bi, hi, qi: (bi, hi, 0, 0)),
                  pl.BlockSpec((1, 1, s_pad, d), lambda bi, hi, qi: (bi, hi, 0, 0))],
        out_specs=pl.BlockSpec((1, 1, tq, d), lambda bi, hi, qi: (bi, hi, qi, 0)),
        out_shape=jax.ShapeDtypeStruct((b, h, lq, d), F32),
        compiler_params=_cparams("parallel", "parallel", "arbitrary"),
        name="sb_attention",
    )(qh, kh, vh)
    return out.transpose(0, 2, 1, 3)


def causal_conv(u, buf, w, bias):
    L = u.shape[1]
    full = jnp.concatenate([buf, u], axis=1)
    out = bias
    for i in range(SSD_CONV):
        out = out + full[:, i:i + L] * w[i]
    return out, full[:, -(SSD_CONV - 1):]


def ssd_scan(x, dt, a, bm, cm, h0):
    b, L, H, P = x.shape
    Q = SSD_CHUNK if L % SSD_CHUNK == 0 else L
    nc = L // Q
    rep = H // SSD_GROUPS
    xf = x.astype(F32).reshape(b, nc, Q, H, P)
    bh = jnp.repeat(bm.astype(F32), rep, axis=2).reshape(b, nc, Q, H, SSD_STATE)
    ch = jnp.repeat(cm.astype(F32), rep, axis=2).reshape(b, nc, Q, H, SSD_STATE)
    dtc = dt.reshape(b, nc, Q, H)
    a_cum = jnp.cumsum(dtc * a, axis=2)
    seg = a_cum[:, :, :, None, :] - a_cum[:, :, None, :, :]
    causal = jnp.tril(jnp.ones((Q, Q), dtype=bool))[None, None, :, :, None]
    decay = jnp.exp(jnp.where(causal, seg, -jnp.inf))
    cb = jnp.einsum('bclhn,bcshn->bclsh', ch, bh)
    y_diag = jnp.einsum('bclsh,bcshp->bclhp', cb * decay * dtc[:, :, None], xf)
    w_state = jnp.exp(a_cum[:, :, -1:] - a_cum) * dtc
    chunk_states = jnp.einsum('bclhn,bclh,bclhp->bchpn', bh, w_state, xf)
    chunk_decay = jnp.exp(a_cum[:, :, -1])

    def step(h, inp):
        dec, st = inp
        return dec[:, :, None, None] * h + st, h

    h_last, h_prev = lax.scan(step, h0, (chunk_decay.transpose(1, 0, 2),
                                          chunk_states.transpose(1, 0, 2, 3, 4)))
    h_prev = h_prev.transpose(1, 0, 2, 3, 4)
    y_off = jnp.einsum('bclhn,bchpn->bclhp', ch, h_prev) * jnp.exp(a_cum)[..., None]
    return (y_diag + y_off).reshape(b, L, H, P), h_last


def hybrid_mixer(xn, k_cache, v_cache, conv_buf, ssd_h,
                 w_in, conv_w, conv_b, dt_bias, a_log, d_skip, ssd_norm, w_out):
    b, L, _ = xn.shape
    proj = xn @ w_in
    q, k, v, z, xbc, dt = jnp.split(proj, HYB_SPLITS, axis=-1)
    q = q.reshape(b, L, SB_HEADS, SB_HEAD_DIM)
    k = k.reshape(b, L, SB_HEADS, SB_HEAD_DIM)
    v = v.reshape(b, L, SB_HEADS, SB_HEAD_DIM)
    past = k_cache.shape[1]
    k_all = jnp.concatenate([k_cache, k], axis=1)
    v_all = jnp.concatenate([v_cache, v], axis=1)
    o_sb = sb_attention(q, k_all, v_all, past).reshape(b, L, SB_WIDTH)
    xbc_c, new_buf = causal_conv(xbc, conv_buf, conv_w, conv_b)
    xbc_c = jax.nn.silu(xbc_c)
    xs, bm, cm = jnp.split(xbc_c, (SSD_INNER, SSD_INNER + SSD_GROUPS * SSD_STATE), axis=-1)
    xs = xs.reshape(b, L, SSD_HEADS, SSD_HEAD_DIM)
    bm = bm.reshape(b, L, SSD_GROUPS, SSD_STATE)
    cm = cm.reshape(b, L, SSD_GROUPS, SSD_STATE)
    dtp = jax.nn.softplus(dt.astype(F32) + dt_bias.astype(F32))
    a = -jnp.exp(a_log.astype(F32))
    y, h_new = ssd_scan(xs, dtp, a, bm, cm, ssd_h.astype(F32))
    y = y + d_skip.astype(F32)[:, None] * xs.astype(F32)
    y = y.reshape(b, L, SSD_INNER) * jax.nn.silu(z.astype(F32))
    yg = y.reshape(b, L, SSD_GROUPS, SSD_INNER // SSD_GROUPS)
    yg = yg * lax.rsqrt(jnp.mean(yg * yg, axis=-1, keepdims=True) + EPS)
    y = yg.reshape(b, L, SSD_INNER) * ssd_norm.astype(F32)
    out = jnp.concatenate([o_sb, y.astype(xn.dtype)], axis=-1) @ w_out
    return out, k, v, new_buf, h_new.astype(ssd_h.dtype)


RW_PAIRS = RW_HEADS // 2
RW_NP = 4
RW_TC = 128


def _rwkv_scan_kernel(r_ref, d_ref, k_ref, v_ref, a_ref, b_ref, s0_ref, o_ref, st_ref, s_scr, vcol_scr, *, tc):
    c = pl.program_id(2)

    @pl.when(c == 0)
    def _():
        s_scr[...] = s0_ref[0]

    row = lax.broadcasted_iota(jnp.int32, (RW_HEAD, 128), 0)
    lane = lax.broadcasted_iota(jnp.int32, (RW_HEAD, 128), 1)
    left = lane < RW_HEAD
    diag = (lane & (RW_HEAD - 1)) == row
    same_head = ((lax.broadcasted_iota(jnp.int32, (128, 128), 0) < RW_HEAD)
                 == (lax.broadcasted_iota(jnp.int32, (128, 128), 1) < RW_HEAD)).astype(jnp.bfloat16)
    same_head2 = jnp.concatenate([same_head] * 2, axis=0)

    def half_sums(x):
        lo = jnp.sum(jnp.where(left, x, 0.0), axis=1, keepdims=True)
        hi = jnp.sum(jnp.where(left, 0.0, x), axis=1, keepdims=True)
        return jnp.where(left, lo, hi)

    def vcol_body(t8, carry):
        base = pl.multiple_of(t8 * 8, 8)
        for p in range(RW_NP):
            terms = _terms3(v_ref[0, pl.ds(base, 8), p * 128:(p + 1) * 128])[:2]
            for i in range(8):
                h, m = (jnp.where(diag, t[i:i + 1], 0.0).astype(jnp.bfloat16) for t in terms)
                vcol_scr[p, base + i] = jnp.dot(jnp.concatenate([h, m], axis=1), same_head2,
                                                preferred_element_type=F32)
        return carry

    lax.fori_loop(0, tc // 8, vcol_body, 0)

    def step8(t8, carry):
        base = pl.multiple_of(t8 * 8, 8)
        lanes = [slice(p * 128, (p + 1) * 128) for p in range(RW_NP)]
        blk = [[ref[0, pl.ds(base, 8), sl] for ref in (a_ref, d_ref, b_ref, k_ref, r_ref)] for sl in lanes]
        state = [s_scr[p] for p in range(RW_NP)]
        rows = [[] for _ in range(RW_NP)]
        for i in range(8):
            for p in range(RW_NP):
                a8, d8, b8, k8, r8 = blk[p]
                s = state[p]
                sa = half_sums(s * a8[i:i + 1])
                s = s * d8[i:i + 1] + sa * b8[i:i + 1] + vcol_scr[p, base + i] * k8[i:i + 1]
                state[p] = s
                h, m, _ = _split3(s * r8[i:i + 1])
                oc = jnp.dot(jnp.concatenate([h, m], axis=1), same_head2, preferred_element_type=F32)
                rows[p].append(jnp.sum(jnp.where(diag, oc, 0.0), axis=0, keepdims=True))
        for p in range(RW_NP):
            s_scr[p] = state[p]
            o_ref[0, pl.ds(base, 8), lanes[p]] = jnp.concatenate(rows[p], axis=0)
        return carry

    lax.fori_loop(0, tc // 8, step8, 0)

    @pl.when(c == pl.num_programs(2) - 1)
    def _():
        st_ref[0] = s_scr[...]


def rwkv_scan(r, d, k, v, a, b, wkv):
    bsz, L, D = r.shape
    tc = min(RW_TC, L)
    s0 = wkv.reshape(bsz, RW_PAIRS, 2, RW_HEAD, RW_HEAD).transpose(0, 1, 3, 2, 4).reshape(bsz, RW_PAIRS, RW_HEAD, 128)
    seq = pl.BlockSpec((1, tc, RW_NP * 128), lambda bi, g, c: (bi, c, g))
    st = pl.BlockSpec((1, RW_NP, RW_HEAD, 128), lambda bi, g, c: (bi, g, 0, 0))
    o, s_new = pl.pallas_call(
        functools.partial(_rwkv_scan_kernel, tc=tc),
        grid=(bsz, RW_PAIRS // RW_NP, L // tc),
        in_specs=[seq] * 6 + [st],
        out_specs=[seq, st],
        out_shape=[jax.ShapeDtypeStruct((bsz, L, D), F32),
                   jax.ShapeDtypeStruct((bsz, RW_PAIRS, RW_HEAD, 128), F32)],
        scratch_shapes=[pltpu.VMEM((RW_NP, RW_HEAD, 128), F32),
                        pltpu.VMEM((RW_NP, tc, RW_HEAD, 128), F32)],
        compiler_params=_cparams("parallel", "parallel", "arbitrary"),
        name="rwkv_scan",
    )(r, d, k, v, a, b, s0)
    s_new = s_new.reshape(bsz, RW_PAIRS, RW_HEAD, 2, RW_HEAD).transpose(0, 1, 3, 2, 4).reshape(bsz, RW_HEADS, RW_HEAD, RW_HEAD)
    return o, s_new


def _proj3_kernel(x_ref, w_ref, o_ref):
    o_ref[0] = jnp.dot(x_ref[0], w_ref[0], preferred_element_type=F32)


def proj3(x3, w3):
    _, m, kdim = x3.shape
    n = w3.shape[2]
    tm = min(m, 512)
    tn = min(n, 512)
    return pl.pallas_call(
        _proj3_kernel,
        grid=(3, m // tm, n // tn),
        in_specs=[pl.BlockSpec((1, tm, kdim), lambda i, a, b: (i, a, 0)),
                  pl.BlockSpec((1, kdim, tn), lambda i, a, b: (i, 0, b))],
        out_specs=pl.BlockSpec((1, tm, tn), lambda i, a, b: (i, a, b)),
        out_shape=jax.ShapeDtypeStruct((3, m, n), F32),
        compiler_params=_cparams("parallel", "parallel", "parallel"),
        name="rwkv_proj3",
    )(x3, w3)


def rwkv7_mixer(xn, shift_buf, wkv, mu, w_rkv, w0, w1, w2, a0, a1, a2, g1, g2,
                k_k, k_a, r_k, lnx_w, lnx_b, w_out):
    b, L, D = xn.shape
    x_prev = jnp.concatenate([shift_buf, xn[:, :-1]], axis=1)
    xx = x_prev - xn
    xm = xn[None] + xx[None] * mu[:, None, None, :]
    r, k, v = proj3(xm[:3].astype(jnp.bfloat16).reshape(3, b * L, D),
                    w_rkv.astype(jnp.bfloat16)).reshape(3, b, L, D)
    xw, xa, xg = xm[3], xm[4], xm[5]
    w = -jax.nn.softplus(-(w0 + jnp.tanh(xw @ w1) @ w2).astype(F32)) - 0.5
    decay = jnp.exp(-jnp.exp(w))
    a = jax.nn.sigmoid((a0 + (xa @ a1) @ a2).astype(F32))
    g = (jax.nn.sigmoid(xg @ g1) @ g2).astype(F32)
    hs = (b, L, RW_HEADS, RW_HEAD)
    kk = (k * k_k.astype(F32)).reshape(hs)
    kk = kk / jnp.maximum(jnp.sqrt(jnp.sum(kk * kk, axis=-1, keepdims=True)), 1e-12)
    k = k * (1.0 + (a - 1.0) * k_a.astype(F32))
    rh, dh, kh, vh, ah = (t.reshape(hs) for t in (r, decay, k, v, a))
    a_vec = -kk
    b_vec = kk * ah
    o, S_last = rwkv_scan(r, decay, k, v, a_vec.reshape(b, L, D), b_vec.reshape(b, L, D), wkv.astype(F32))
    o = o.reshape(hs)
    mean = jnp.mean(o, axis=-1, keepdims=True)
    var = jnp.mean(jnp.square(o - mean), axis=-1, keepdims=True)
    o = ((o - mean) * lax.rsqrt(var + RW_LNX_EPS)).reshape(b, L, D)
    o = o * lnx_w.astype(F32) + lnx_b.astype(F32)
    bonus = jnp.sum(rh * kh * r_k.astype(F32), axis=-1, keepdims=True) * vh
    o = o + bonus.reshape(b, L, D)
    out = (o * g).astype(xn.dtype) @ w_out
    return out, xn[:, -1:], S_last.astype(wkv.dtype)


PEER_TM = 256
PEER_TB = 64
PEER_NSTAGE = 16
PEER_HALF = D_MODEL // 2
PEER_SUB = PEER_HALF // 128
PEER_SEL = PEER_HEADS * PEER_TOPK


def _peer_scores_kernel(x_ref, g_ref, wq_ref, keys_ref, s_ref, xn_ref, xnb_ref):
    @pl.when(pl.program_id(1) == 0)
    def _():
        x = x_ref[...]
        xn = x * lax.rsqrt(jnp.mean(x * x, axis=-1, keepdims=True) + EPS) * g_ref[...]
        xn_ref[...] = xn
        xnb_ref[...] = xn.astype(jnp.bfloat16)

    q = jnp.dot(xnb_ref[...], wq_ref[...], preferred_element_type=F32).astype(jnp.bfloat16)
    half = PEER_QDIM // 2
    for i in range(2):
        s_ref[i * PEER_NKEYS:(i + 1) * PEER_NKEYS, :] = lax.dot_general(
            keys_ref[0, i], q[:, i * half:(i + 1) * half],
            (((1,), (1,)), ((), ())), preferred_element_type=F32)


def peer_scores(x, g, wq_bf, keys_bf):
    t, d = x.shape
    return pl.pallas_call(
        _peer_scores_kernel,
        grid=(t // PEER_TM, PEER_HEADS),
        in_specs=[pl.BlockSpec((PEER_TM, d), lambda i, h: (i, 0)),
                  pl.BlockSpec((1, d), lambda i, h: (0, 0)),
                  pl.BlockSpec((d, PEER_QDIM), lambda i, h: (0, h)),
                  pl.BlockSpec((1, 2, PEER_NKEYS, PEER_QDIM // 2), lambda i, h: (h, 0, 0, 0))],
        out_specs=[pl.BlockSpec((2 * PEER_NKEYS, PEER_TM), lambda i, h: (h, i)),
                   pl.BlockSpec((PEER_TM, d), lambda i, h: (i, 0))],
        out_shape=[jax.ShapeDtypeStruct((PEER_HEADS * 2 * PEER_NKEYS, t), F32),
                   jax.ShapeDtypeStruct((t, d), F32)],
        scratch_shapes=[pltpu.VMEM((PEER_TM, d), jnp.bfloat16)],
        compiler_params=_cparams("parallel", "arbitrary"),
        name="peer_scores",
    )(x, g.reshape(1, d), wq_bf, keys_bf)


PEER_TT = 128
NEG_INF = float("-inf")


def _peer_candidates():
    k = PEER_TOPK
    a_idx, b_idx, valid = [], [], []
    for a in range(k // 2):
        nb = k // (a + 1)
        rows = k if a == 0 else 8
        for b in range(rows):
            a_idx.append(a); b_idx.append(b); valid.append(b < nb)
    for a in range(k // 2, k):
        a_idx.append(a); b_idx.append(0); valid.append(True)
    return np.array(a_idx), np.array(b_idx), np.array(valid)


_CAND_A, _CAND_B, _CAND_VALID = _peer_candidates()
PEER_NCAND = len(_CAND_A)


def _top_rows(xs, tag, k):
    big = jnp.float32(1e9)
    xs = list(xs)
    vals = [[] for _ in xs]
    tags = [[] for _ in xs]
    for _ in range(k):
        for n, x in enumerate(xs):
            m = jnp.max(x, axis=0, keepdims=True)
            t = jnp.min(jnp.where(x == m, tag, big), axis=0, keepdims=True)
            vals[n].append(m)
            tags[n].append(t)
            xs[n] = jnp.where(tag == t, NEG_INF, x)
    return vals, tags


def _peer_topk_kernel(s_ref, pos_ref, eid_ref, gate_ref, v_scr, i_scr, e_scr, g_scr):
    k = PEER_TOPK
    key_tag = lax.broadcasted_iota(jnp.int32, (PEER_NKEYS, PEER_TT), 0).astype(F32)

    def half_body(h, carry):
        starts = [pl.multiple_of((2 * h + i) * PEER_NKEYS, PEER_NKEYS) for i in range(2)]
        vals, tags = _top_rows([s_ref[pl.ds(st, PEER_NKEYS), :] for st in starts], key_tag, k)
        for i in range(2):
            v_scr[2 * h + i] = jnp.concatenate(vals[i], axis=0)
            i_scr[2 * h + i] = jnp.concatenate(tags[i], axis=0)
        return carry

    lax.fori_loop(0, PEER_HEADS, half_body, 0)

    pos = pos_ref[...]
    valid = pos >= 0.0

    def candidates(h):
        s1 = v_scr[2 * h]; s2 = v_scr[2 * h + 1]
        i1 = i_scr[2 * h]; i2 = i_scr[2 * h + 1]
        e2 = i2[0:8]
        pieces_s = [s1[0:1] + s2]
        pieces_e = [i1[0:1] * PEER_NKEYS + i2]
        for a in range(1, k // 2):
            pieces_s.append(s1[a:a + 1] + s2[0:8])
            pieces_e.append(i1[a:a + 1] * PEER_NKEYS + e2)
        pieces_s.append(s1[k // 2:k] + s2[0:1])
        pieces_e.append(i1[k // 2:k] * PEER_NKEYS + i2[0:1])
        return jnp.where(valid, jnp.concatenate(pieces_s, axis=0), NEG_INF), jnp.concatenate(pieces_e, axis=0)

    def head_pair_body(hp, carry):
        heads = [2 * hp, 2 * hp + 1]
        cands, cids = zip(*[candidates(h) for h in heads])
        tops, tags = _top_rows(cands, pos, k)
        for n, h in enumerate(heads):
            eids = [jnp.max(jnp.where(pos == t, cids[n], -1.0), axis=0, keepdims=True) for t in tags[n]]
            top = jnp.concatenate(tops[n], axis=0)
            p = jnp.exp(top - top[0:1])
            gate = p / jnp.sum(p, axis=0, keepdims=True)
            row = pl.multiple_of(h * k, k)
            e_scr[pl.ds(row, k), :] = jnp.concatenate(eids, axis=0)
            g_scr[pl.ds(row, k), :] = gate
        return carry

    lax.fori_loop(0, PEER_HEADS // 2, head_pair_body, 0)
    eid_ref[...] = e_scr[...].T
    gate_ref[...] = g_scr[...].T


def peer_topk(scores_t):
    t = scores_t.shape[1]
    pos = np.where(_CAND_VALID, _CAND_A * PEER_TOPK + _CAND_B, -1).astype(np.float32)
    pos = jnp.asarray(np.broadcast_to(pos[:, None], (PEER_NCAND, PEER_TT)))
    return pl.pallas_call(
        _peer_topk_kernel,
        grid=(t // PEER_TT,),
        in_specs=[pl.BlockSpec((2 * PEER_HEADS * PEER_NKEYS, PEER_TT), lambda i: (0, i)),
                  pl.BlockSpec((PEER_NCAND, PEER_TT), lambda i: (0, 0))],
        out_specs=[pl.BlockSpec((PEER_TT, PEER_SEL), lambda i: (i, 0)),
                   pl.BlockSpec((PEER_TT, PEER_SEL), lambda i: (i, 0))],
        out_shape=[jax.ShapeDtypeStruct((t, PEER_SEL), F32),
                   jax.ShapeDtypeStruct((t, PEER_SEL), F32)],
        scratch_shapes=[pltpu.VMEM((2 * PEER_HEADS, PEER_TOPK, PEER_TT), F32),
                        pltpu.VMEM((2 * PEER_HEADS, PEER_TOPK, PEER_TT), F32),
                        pltpu.VMEM((PEER_SEL, PEER_TT), F32),
                        pltpu.VMEM((PEER_SEL, PEER_TT), F32)],
        compiler_params=_cparams("parallel"),
        name="peer_topk",
    )(scores_t, pos)


def pack_table(tab):
    e = tab.shape[0]
    b = lax.bitcast_convert_type(tab.astype(jnp.bfloat16), jnp.uint16).astype(jnp.uint32)
    b = b.reshape(e, 2, PEER_SUB, 128)
    return (b[:, 0] << 16) | b[:, 1]


def _terms3(x):
    h = x.astype(jnp.bfloat16).astype(F32)
    r = x - h
    m = r.astype(jnp.bfloat16).astype(F32)
    l = (r - m).astype(jnp.bfloat16).astype(F32)
    return h, m, l


def _stack_terms(terms, s):
    row = lax.broadcasted_iota(jnp.int32, (8, 128), 0)
    h, m, l = (t[s:s + 1] for t in terms)
    return jnp.where(row == 0, h, jnp.where(row == 1, m, jnp.where(row == 2, l, 0.0))).astype(jnp.bfloat16)


def _stage_rows(eid_ref, tab_ref, w_scr, t):
    for j in range(PEER_SEL):
        w_scr[PEER_SUB * j:PEER_SUB * (j + 1), :] = tab_ref[eid_ref[t, j]]


def _token_groups(eid_ref, tab_ref, w_scrs, compute):
    n = len(w_scrs)

    def group(i, carry):
        for k in range(n):
            _stage_rows(eid_ref, tab_ref, w_scrs[k], n * i + k)
        for k in range(n):
            compute(n * i + k, w_scrs[k])
        return carry

    lax.fori_loop(0, PEER_TB // n, group, 0)


def _halves(w_scr, s):
    w = w_scr[pl.ds(s, PEER_SEL, stride=PEER_SUB), :]
    hi = lax.bitcast_convert_type(w & jnp.uint32(0xFFFF0000), F32).astype(jnp.bfloat16)
    lo = lax.bitcast_convert_type(w << 16, F32).astype(jnp.bfloat16)
    return hi, lo


def _peer_hidden_kernel(eid_ref, x_ref, tab_ref, hid_ref, *w_scrs):
    nt = (((1,), (1,)), ((), ()))

    def compute(t, w_scr):
        th = _terms3(x_ref[t, 0])
        tl = _terms3(x_ref[t, 1])
        acc = jnp.zeros((8, PEER_SEL), F32)
        for s in range(PEER_SUB):
            hi, lo = _halves(w_scr, s)
            acc = acc + lax.dot_general(_stack_terms(th, s), hi, nt, preferred_element_type=F32)
            acc = acc + lax.dot_general(_stack_terms(tl, s), lo, nt, preferred_element_type=F32)
        hid_ref[t] = acc[0:1] + acc[1:2] + acc[2:3]

    _token_groups(eid_ref, tab_ref, w_scrs, compute)


def _peer_out_kernel(eid_ref, act_ref, res_ref, tab_ref, out_ref, *w_scrs):
    def compute(t, w_scr):
        a8 = _stack_terms(_terms3(act_ref[t]), 0)
        for s in range(PEER_SUB):
            hi, lo = _halves(w_scr, s)
            yh = jnp.dot(a8, hi, preferred_element_type=F32)
            yl = jnp.dot(a8, lo, preferred_element_type=F32)
            out_ref[t, 0, s:s + 1, :] = res_ref[t, 0, s:s + 1, :] + (yh[0:1] + yh[1:2] + yh[2:3])
            out_ref[t, 1, s:s + 1, :] = res_ref[t, 1, s:s + 1, :] + (yl[0:1] + yl[1:2] + yl[2:3])

    _token_groups(eid_ref, tab_ref, w_scrs, compute)


def _table_spec(n_exp):
    return pl.BlockSpec((n_exp, PEER_SUB, 128), lambda i: (0, 0, 0), pipeline_mode=pl.Buffered(1))


def peer_hidden(eid, xn_split, tab_packed):
    t = eid.shape[0]
    n_exp = tab_packed.shape[0]
    return pl.pallas_call(
        _peer_hidden_kernel,
        grid=(t // PEER_TB,),
        in_specs=[pl.BlockSpec((PEER_TB, PEER_SEL), lambda i: (i, 0), memory_space=pltpu.SMEM),
                  pl.BlockSpec((PEER_TB, 2, PEER_SUB, 128), lambda i: (i, 0, 0, 0)),
                  _table_spec(n_exp)],
        out_specs=pl.BlockSpec((PEER_TB, 1, PEER_SEL), lambda i: (i, 0, 0)),
        out_shape=jax.ShapeDtypeStruct((t, 1, PEER_SEL), F32),
        scratch_shapes=[pltpu.VMEM((PEER_SEL * PEER_SUB, 128), jnp.uint32)] * PEER_NSTAGE,
        compiler_params=_cparams("parallel"),
        name="peer_hidden",
    )(eid, xn_split, tab_packed)


def peer_out(eid, act, res_split, tab_packed):
    t = eid.shape[0]
    n_exp = tab_packed.shape[0]
    return pl.pallas_call(
        _peer_out_kernel,
        grid=(t // PEER_TB,),
        in_specs=[pl.BlockSpec((PEER_TB, PEER_SEL), lambda i: (i, 0), memory_space=pltpu.SMEM),
                  pl.BlockSpec((PEER_TB, 1, PEER_SEL), lambda i: (i, 0, 0)),
                  pl.BlockSpec((PEER_TB, 2, PEER_SUB, 128), lambda i: (i, 0, 0, 0)),
                  _table_spec(n_exp)],
        out_specs=pl.BlockSpec((PEER_TB, 2, PEER_SUB, 128), lambda i: (i, 0, 0, 0)),
        out_shape=jax.ShapeDtypeStruct((t, 2, PEER_SUB, 128), F32),
        scratch_shapes=[pltpu.VMEM((PEER_SEL * PEER_SUB, 128), jnp.uint32)] * PEER_NSTAGE,
        compiler_params=_cparams("parallel"),
        name="peer_out",
    )(eid, act.reshape(t, 1, PEER_SEL), res_split, tab_packed)


def peer_layer(x, ng, w_q, keys, u_packed, v_packed):
    t, d = x.shape
    scores, xn = peer_scores(x, ng, w_q.astype(jnp.bfloat16), keys.astype(jnp.bfloat16))
    eid_f, gate = peer_topk(scores)
    eid = eid_f.astype(jnp.int32)
    hidden = peer_hidden(eid, xn.reshape(t, 2, PEER_SUB, 128), u_packed).reshape(t, PEER_SEL)
    act = jax.nn.gelu(hidden, approximate=False) * gate
    out = peer_out(eid, act, x.reshape(t, 2, PEER_SUB, 128), v_packed)
    return out.reshape(t, d)


def mixer_layer(l, x, states, mp):
    xn = rmsnorm(x, mp[0])
    if l % 2 == 0:
        out, *ns = hybrid_mixer(xn, *states, *mp[1:])
    else:
        out, *ns = rwkv7_mixer(xn, *states, *mp[1:])
    return x + out, ns


def run_trunks(xs, states_list, layer_params, peer_params, final_norm):
    new_states = [[] for _ in xs]
    sizes = [x.shape[0] * x.shape[1] for x in xs]
    offs = np.cumsum([0] + sizes)
    for l in range(DEPTH):
        mixed = []
        for gi, x in enumerate(xs):
            x, ns = mixer_layer(l, x, states_list[gi][l], layer_params[l])
            new_states[gi].extend(ns)
            mixed.append(x.reshape(-1, D_MODEL))
        ng, wq, keys, u_packed, v_packed = peer_params[l]
        y = peer_layer(jnp.concatenate(mixed, axis=0), ng, wq, keys, u_packed, v_packed)
        xs = [y[offs[gi]:offs[gi + 1]].reshape(xs[gi].shape) for gi in range(len(xs))]
    ys = [pallas_rmsnorm(x, final_norm) for x in xs]
    return ys, new_states


def kernel(x_prompt, x_sample, cache_sb_k, cache_sb_v, state_ssd_conv, state_ssd,
           state_rwkv_shift, state_rwkv_wkv,
           l0_norm_mix, l0_w_in, l0_conv_w, l0_conv_b, l0_dt_bias, l0_a_log, l0_d_skip,
           l0_ssd_norm, l0_w_out,
           l1_norm_mix, l1_mu, l1_w_rkv, l1_w0, l1_w1, l1_w2, l1_a0, l1_a1, l1_a2,
           l1_g1, l1_g2, l1_k_k, l1_k_a, l1_r_k, l1_lnx_w, l1_lnx_b, l1_w_out,
           l0_norm_ffn, l0_peer_wq, l0_peer_keys, l0_peer_u, l0_peer_v,
           l1_norm_ffn, l1_peer_wq, l1_peer_keys, l1_peer_u, l1_peer_v,
           final_norm):
    layer_params = [
        (l0_norm_mix, l0_w_in, l0_conv_w, l0_conv_b, l0_dt_bias, l0_a_log, l0_d_skip,
         l0_ssd_norm, l0_w_out),
        (l1_norm_mix, l1_mu, l1_w_rkv, l1_w0, l1_w1, l1_w2, l1_a0, l1_a1, l1_a2,
         l1_g1, l1_g2, l1_k_k, l1_k_a, l1_r_k, l1_lnx_w, l1_lnx_b, l1_w_out),
    ]
    peer_params = [
        (l0_norm_ffn, l0_peer_wq, l0_peer_keys, pack_table(l0_peer_u), pack_table(l0_peer_v)),
        (l1_norm_ffn, l1_peer_wq, l1_peer_keys, pack_table(l1_peer_u), pack_table(l1_peer_v)),
    ]
    bp = x_prompt.shape[0]
    dt = x_prompt.dtype
    zero_states = [
        (jnp.zeros((bp, 0, SB_HEADS, SB_HEAD_DIM), dt), jnp.zeros((bp, 0, SB_HEADS, SB_HEAD_DIM), dt),
         jnp.zeros((bp, SSD_CONV - 1, SSD_CONV_DIM), dt),
         jnp.zeros((bp, SSD_HEADS, SSD_HEAD_DIM, SSD_STATE), dt)),
        (jnp.zeros((bp, 1, D_MODEL), dt), jnp.zeros((bp, RW_HEADS, RW_HEAD, RW_HEAD), dt)),
    ]
    sample_states = [
        (cache_sb_k, cache_sb_v, state_ssd_conv, state_ssd),
        (state_rwkv_shift, state_rwkv_wkv),
    ]
    (y_prompt, y_sample), (p_states, s_states) = run_trunks(
        [x_prompt, x_sample], [zero_states, sample_states], layer_params, peer_params, final_norm)
    p_sb_k, p_sb_v, p_conv, p_ssd, p_shift, p_wkv = p_states
    s_sb_k, s_sb_v, s_conv, s_ssd, s_shift, s_wkv = s_states
    return (y_prompt, y_sample, p_sb_k, p_sb_v, p_conv, p_ssd, p_shift, p_wkv,
            s_sb_k, s_sb_v, s_conv, s_ssd, s_shift, s_wkv)
```
